```python
import jax, jax.numpy as jnp
from jax import lax
import numpy as np

D_MODEL = 1024
BATCH = 8
SEQ = 2048
DEPTH = 2

CTX_LEN = 256
GRID_W = 64
MLA_HEADS = 4
QK_NOPE = 128
QK_ROPE = 64
V_DIM = 128
Q_LORA = 384
KV_LORA = 256
MLA_W = MLA_HEADS * V_DIM
FNET_GROUPS = 4
FNET_W = 256
CONV_GROUPS = 4
CONV_W = 256
D_MIX = MLA_W + FNET_W + CONV_W
MLA_IN = Q_LORA + KV_LORA + QK_ROPE
D_IN = MLA_IN + FNET_W + 3 * CONV_W
D_FF = 2816
ROPE_THETA = 10000.0
Q_BLOCK = 128
EPS = 1e-6
SM_SCALE = (QK_NOPE + QK_ROPE) ** -0.5

kernel_name = "hymba_style_mla_fnet_shortconv_dit_block"


def rmsnorm(x, g):
    xf = x.astype(jnp.float32)
    y = xf * lax.rsqrt(jnp.mean(xf * xf, axis=-1, keepdims=True) + EPS)
    return (y * g.astype(jnp.float32)).astype(x.dtype)


def modulate(h, shift, scale):
    return h * (1 + scale) + shift


def dwconv3(x, w, b):
    xp = jnp.pad(x, ((0, 0), (1, 1), (0, 0)))
    return xp[:, :-2] * w[0] + xp[:, 1:-1] * w[1] + xp[:, 2:] * w[2] + b


def axial_rope_tables(n_tokens):
    rows = n_tokens // GRID_W
    r = jnp.repeat(jnp.arange(rows), GRID_W).astype(jnp.float32)
    cidx = jnp.tile(jnp.arange(GRID_W), rows).astype(jnp.float32)
    half = QK_ROPE // 2
    inv = ROPE_THETA ** (-jnp.arange(0, half, 2, dtype=jnp.float32) / half)
    ang = jnp.stack([r[:, None] * inv, cidx[:, None] * inv], axis=1)
    return jnp.cos(ang), jnp.sin(ang)


def apply_axial_rope(x, cos, sin):
    shp = x.shape
    xf = x.astype(jnp.float32).reshape(shp[:-1] + (2, 2, QK_ROPE // 4))
    x1, x2 = xf[..., 0, :], xf[..., 1, :]
    out = jnp.stack([x1 * cos - x2 * sin, x1 * sin + x2 * cos], axis=-2)
    return out.reshape(shp).astype(x.dtype)


def mla_project(p, w_uq, q_norm_g, w_ukv, kv_norm_g, rope):
    cq = rmsnorm(p[..., :Q_LORA], q_norm_g)
    ckv = rmsnorm(p[..., Q_LORA:Q_LORA + KV_LORA], kv_norm_g)
    k_rope = p[..., Q_LORA + KV_LORA:]
    q = jnp.einsum('blr,rhd->blhd', cq, w_uq)
    kv = jnp.einsum('blr,rhd->blhd', ckv, w_ukv)
    q_nope, q_rope = q[..., :QK_NOPE], q[..., QK_NOPE:]
    k_nope, v = kv[..., :QK_NOPE], kv[..., QK_NOPE:]
    if rope is not None:
        cos, sin = rope
        q_rope = apply_axial_rope(q_rope, cos[:, None], sin[:, None])
        k_rope = apply_axial_rope(k_rope, cos, sin)
    return q_nope, q_rope, k_nope, k_rope, v


def mla_attend(q_n, q_r, k_n, k_r, v):
    s = (jnp.einsum('bqhd,bkhd->bhqk', q_n, k_n)
         + jnp.einsum('bqhr,bkr->bhqk', q_r, k_r)).astype(jnp.float32) * SM_SCALE
    p = jax.nn.softmax(s, axis=-1).astype(v.dtype)
    return jnp.einsum('bhqk,bkhd->bqhd', p, v)


def mla_attend_blocked(q_n, q_r, k_n, k_r, v):
    B, L, H, _ = q_n.shape
    nb = L // Q_BLOCK
    qn_b = jnp.moveaxis(q_n.reshape(B, nb, Q_BLOCK, H, QK_NOPE), 1, 0)
    qr_b = jnp.moveaxis(q_r.reshape(B, nb, Q_BLOCK, H, QK_ROPE), 1, 0)
    out = lax.map(lambda qs: mla_attend(qs[0], qs[1], k_n, k_r, v), (qn_b, qr_b))
    return jnp.moveaxis(out, 0, 1).reshape(B, L, H * V_DIM)


def fourier_mix(u):
    B, L, _ = u.shape
    ug = u.astype(jnp.float32).reshape(B, L, FNET_GROUPS, FNET_W // FNET_GROUPS)
    y = jnp.fft.fft2(ug, axes=(1, 3), norm='ortho').real
    return y.reshape(B, L, FNET_W).astype(u.dtype)


def short_gated_conv(p, w, b):
    bg, cg, xv = p[..., :CONV_W], p[..., CONV_W:2 * CONV_W], p[..., 2 * CONV_W:]
    return bg * dwconv3(cg * xv, w, b)


def merge_groups(att, p_rest, sconv_w, sconv_b, out_norm_g, w_out):
    y_f = fourier_mix(p_rest[..., :FNET_W])
    y_c = short_gated_conv(p_rest[..., FNET_W:], sconv_w, sconv_b)
    y = jnp.concatenate([rmsnorm(att, out_norm_g[:MLA_W]),
                         rmsnorm(y_f, out_norm_g[MLA_W:MLA_W + FNET_W]),
                         rmsnorm(y_c, out_norm_g[MLA_W + FNET_W:])], axis=-1)
    return y @ w_out


def conv_ffn(x, shift, scale, g, w_up, cw, cb, w_down):
    h = modulate(rmsnorm(x, g), shift, scale)
    u = h @ w_up
    gate = dwconv3(u[..., :D_FF], cw, cb)
    return (jax.nn.silu(gate) * u[..., D_FF:]) @ w_down


def setup_inputs(seed: int = 0) -> dict:
    key = jax.random.key(seed)
    ks = jax.random.split(key, 24)
    f32 = jnp.float32
    nrm = lambda k, shp, s: jax.random.normal(k, shp, f32) * s
    L = DEPTH
    return {
        "x": nrm(ks[0], (BATCH, SEQ, D_MODEL), 1.0),
        "c": nrm(ks[1], (BATCH, D_MODEL), 1.0),
        "ctx": nrm(ks[2], (BATCH, CTX_LEN, D_MODEL), 1.0),
        "c_ctx": nrm(ks[3], (D_MODEL,), 1.0),
        "ada_w": nrm(ks[4], (L, D_MODEL, 6 * D_MODEL), D_MODEL ** -0.5),
        "ada_b": nrm(ks[5], (L, 6 * D_MODEL), 0.02),
        "norm1_g": 1.0 + nrm(ks[6], (L, D_MODEL), 0.02),
        "w_in": nrm(ks[7], (L, D_MODEL, D_IN), D_MODEL ** -0.5),
        "q_norm_g": 1.0 + nrm(ks[8], (L, Q_LORA), 0.02),
        "kv_norm_g": 1.0 + nrm(ks[9], (L, KV_LORA), 0.02),
        "w_uq": nrm(ks[10], (L, Q_LORA, MLA_HEADS, QK_NOPE + QK_ROPE), Q_LORA ** -0.5),
        "w_ukv": nrm(ks[11], (L, KV_LORA, MLA_HEADS, QK_NOPE + V_DIM), KV_LORA ** -0.5),
        "sconv_w": nrm(ks[12], (L, 3, CONV_W), 3 ** -0.5),
        "sconv_b": nrm(ks[13], (L, CONV_W), 0.02),
        "out_norm_g": 1.0 + nrm(ks[14], (L, D_MIX), 0.02),
        "w_out": nrm(ks[15], (L, D_MIX, D_MODEL), D_MIX ** -0.5),
        "norm2_g": 1.0 + nrm(ks[16], (L, D_MODEL), 0.02),
        "w_up": nrm(ks[17], (L, D_MODEL, 2 * D_FF), D_MODEL ** -0.5),
        "ffconv_w": nrm(ks[18], (L, 3, D_FF), 3 ** -0.5),
        "ffconv_b": nrm(ks[19], (L, D_FF), 0.02),
        "w_down": nrm(ks[20], (L, D_FF, D_MODEL), D_FF ** -0.5),
        "final_g": 1.0 + nrm(ks[21], (D_MODEL,), 0.02),
    }


def reference(x, c, ctx, c_ctx, ada_w, ada_b, norm1_g, w_in, q_norm_g, kv_norm_g, w_uq, w_ukv,
              sconv_w, sconv_b, out_norm_g, w_out, norm2_g, w_up, ffconv_w, ffconv_b, w_down,
              final_g):
    B, S, _ = x.shape
    rope = axial_rope_tables(S)
    x_lat, x_ctx = x, ctx
    for l in range(DEPTH):
        last = l == DEPTH - 1
        mod_l = (jax.nn.silu(c) @ ada_w[l] + ada_b[l])[:, None, :]
        mod_c = (jax.nn.silu(c_ctx) @ ada_w[l] + ada_b[l])[None, None, :]
        sh1, sc1, ga1, sh2, sc2, ga2 = jnp.split(mod_l, 6, axis=-1)
        csh1, csc1, cga1, csh2, csc2, cga2 = jnp.split(mod_c, 6, axis=-1)

        p_lat = modulate(rmsnorm(x_lat, norm1_g[l]), sh1, sc1) @ w_in[l]
        p_ctx = modulate(rmsnorm(x_ctx, norm1_g[l]), csh1, csc1) @ w_in[l]
        ql_n, ql_r, kl_n, kl_r, vl = mla_project(p_lat[..., :MLA_IN], w_uq[l], q_norm_g[l],
                                                 w_ukv[l], kv_norm_g[l], rope)
        qc_n, qc_r, kc_n, kc_r, vc = mla_project(p_ctx[..., :MLA_IN], w_uq[l], q_norm_g[l],
                                                 w_ukv[l], kv_norm_g[l], None)
        k_n = jnp.concatenate([kc_n, kl_n], axis=1)
        k_r = jnp.concatenate([kc_r, kl_r], axis=1)
        v = jnp.concatenate([vc, vl], axis=1)
        att_lat = mla_attend_blocked(ql_n, ql_r, k_n, k_r, v)
        y_lat = merge_groups(att_lat, p_lat[..., MLA_IN:], sconv_w[l], sconv_b[l],
                             out_norm_g[l], w_out[l])
        x_lat_new = x_lat + ga1 * y_lat
        if not last:
            att_ctx = mla_attend(qc_n, qc_r, kc_n, kc_r, vc).reshape(B, -1, MLA_W)
            y_ctx = merge_groups(att_ctx, p_ctx[..., MLA_IN:], sconv_w[l], sconv_b[l],
                                 out_norm_g[l], w_out[l])
            x_ctx = x_ctx + cga1 * y_ctx
        x_lat = x_lat_new

        x_lat = x_lat + ga2 * conv_ffn(x_lat, sh2, sc2, norm2_g[l], w_up[l], ffconv_w[l],
                                       ffconv_b[l], w_down[l])
        if not last:
            x_ctx = x_ctx + cga2 * conv_ffn(x_ctx, csh2, csc2, norm2_g[l], w_up[l], ffconv_w[l],
                                            ffconv_b[l], w_down[l])
    return rmsnorm(x_lat, final_g)
```

```python
import functools

import jax
import jax.numpy as jnp
import numpy as np
from jax import lax
from jax.experimental import pallas as pl
from jax.experimental.pallas import tpu as pltpu

D_MODEL = 1024
DEPTH = 2
GRID_W = 64
MLA_HEADS = 4
QK_NOPE = 128
QK_ROPE = 64
V_DIM = 128
Q_LORA = 384
KV_LORA = 256
MLA_W = MLA_HEADS * V_DIM
FNET_GROUPS = 4
FNET_W = 256
CONV_W = 256
D_MIX = MLA_W + FNET_W + CONV_W
D_FF = 2816
ROPE_THETA = 10000.0
EPS = 1e-6
SM_SCALE = (QK_NOPE + QK_ROPE) ** -0.5

LANE = 128
ROPE_TILE = 2 * QK_ROPE
P_COLS = Q_LORA + KV_LORA + ROPE_TILE + FNET_W + 3 * CONV_W
Q_COLS = MLA_HEADS * QK_NOPE + MLA_HEADS * QK_ROPE
KV_COLS = MLA_HEADS * QK_NOPE + ROPE_TILE + MLA_HEADS * V_DIM
REST_COLS = FNET_W + 3 * CONV_W
MOD_ROWS = 16
VMEM_LIMIT = 56 * 1024 * 1024

BF16 = jnp.bfloat16
F32 = jnp.float32


def _params(sem):
    return pltpu.CompilerParams(dimension_semantics=sem, vmem_limit_bytes=VMEM_LIMIT)


def _rms(x, g):
    return x * lax.rsqrt(jnp.mean(x * x, axis=-1, keepdims=True) + EPS) * g


def _dot(a, b):
    return jnp.dot(a, b, preferred_element_type=F32)


def _conv3_rows(z, w_ref, b_ref, seq_len):
    n = z.shape[0]
    row = lax.broadcasted_iota(jnp.int32, z.shape, 0) % seq_len
    prev = jnp.where(row == 0, 0.0, pltpu.roll(z, 1, axis=0))
    nxt = jnp.where(row == seq_len - 1, 0.0, pltpu.roll(z, n - 1, axis=0))
    return prev * w_ref[0:1, :] + z * w_ref[1:2, :] + nxt * w_ref[2:3, :] + b_ref[...]


def _ada_kernel(c_ref, w_ref, b_ref, o_ref):
    c = c_ref[...]
    s = (c * jax.nn.sigmoid(c)).astype(BF16)
    o_ref[0] = _dot(s, w_ref[0].astype(BF16)) + b_ref[0]


def _ada_mod(cond, ada_w, ada_b):
    n_col = 6 * D_MODEL
    bc = 1536
    return pl.pallas_call(
        _ada_kernel,
        grid=(DEPTH, n_col // bc),
        in_specs=[
            pl.BlockSpec((MOD_ROWS, D_MODEL), lambda l, j: (0, 0)),
            pl.BlockSpec((1, D_MODEL, bc), lambda l, j: (l, 0, j)),
            pl.BlockSpec((1, 1, bc), lambda l, j: (l, 0, j)),
        ],
        out_specs=pl.BlockSpec((1, MOD_ROWS, bc), lambda l, j: (l, 0, j)),
        out_shape=jax.ShapeDtypeStruct((DEPTH, MOD_ROWS, n_col), F32),
        compiler_params=_params(("arbitrary", "arbitrary")),
        name="ada_mod",
    )(cond, ada_w, ada_b.reshape(DEPTH, 1, n_col))


def _rope_tile(t, tab_ref):
    return (t * tab_ref[:, 0:LANE]
            + pltpu.roll(t, QK_ROPE // 2, axis=1) * tab_ref[:, LANE:2 * LANE]
            + pltpu.roll(t, LANE - QK_ROPE // 2, axis=1) * tab_ref[:, 2 * LANE:3 * LANE])


def _inproj_kernel(*refs, use_rope):
    if use_rope:
        (x_ref, mod_ref, g1_ref, win_ref, qg_ref, kvg_ref, wq_ref, wkv_ref, tab_ref,
         q_ref, kv_ref, rest_ref) = refs
    else:
        (x_ref, mod_ref, g1_ref, win_ref, qg_ref, kvg_ref, wq_ref, wkv_ref,
         q_ref, kv_ref, rest_ref) = refs
    x = x_ref[0]
    h = _rms(x, g1_ref[...]) * (1.0 + mod_ref[0, 1:2, :]) + mod_ref[0, 0:1, :]
    p = _dot(h.astype(BF16), win_ref[...])
    cq = _rms(p[:, :Q_LORA], qg_ref[...]).astype(BF16)
    ckv = _rms(p[:, Q_LORA:Q_LORA + KV_LORA], kvg_ref[...]).astype(BF16)
    q = _dot(cq, wq_ref[...])
    kv = _dot(ckv, wkv_ref[...])
    n_nope = MLA_HEADS * QK_NOPE
    kr = p[:, Q_LORA + KV_LORA:Q_LORA + KV_LORA + ROPE_TILE]
    qr = [q[:, n_nope + i * LANE:n_nope + (i + 1) * LANE] for i in range(2)]
    if use_rope:
        kr = _rope_tile(kr, tab_ref)
        qr = [_rope_tile(t, tab_ref) for t in qr]
    q_ref[0, :, :n_nope] = (q[:, :n_nope] * SM_SCALE).astype(BF16)
    for i in range(2):
        q_ref[0, :, n_nope + i * LANE:n_nope + (i + 1) * LANE] = (qr[i] * SM_SCALE).astype(BF16)
    kv_ref[0, :, :n_nope] = kv[:, :n_nope].astype(BF16)
    kv_ref[0, :, n_nope:n_nope + ROPE_TILE] = kr.astype(BF16)
    kv_ref[0, :, n_nope + ROPE_TILE:] = kv[:, n_nope:].astype(BF16)
    rest_ref[0] = p[:, Q_LORA + KV_LORA + ROPE_TILE:].astype(BF16)


def _inproj(x, mod, g1, win, qg, kvg, wq, wkv, rope_tab, tr):
    nb, n_rows, _ = x.shape
    use_rope = rope_tab is not None
    const = lambda b, t: (0, 0)
    in_specs = [
        pl.BlockSpec((1, tr, D_MODEL), lambda b, t: (b, t, 0)),
        pl.BlockSpec((1, 6, D_MODEL), lambda b, t: (b, 0, 0)),
        pl.BlockSpec((1, D_MODEL), const),
        pl.BlockSpec((D_MODEL, P_COLS), const),
        pl.BlockSpec((1, Q_LORA), const),
        pl.BlockSpec((1, KV_LORA), const),
        pl.BlockSpec((Q_LORA, Q_COLS), const),
        pl.BlockSpec((KV_LORA, 2 * MLA_W), const),
    ]
    args = [x, mod, g1, win, qg, kvg, wq, wkv]
    if use_rope:
        in_specs.append(pl.BlockSpec((tr, 3 * LANE), lambda b, t: (t, 0)))
        args.append(rope_tab)
    out_cols = (Q_COLS, KV_COLS, REST_COLS)
    return pl.pallas_call(
        functools.partial(_inproj_kernel, use_rope=use_rope),
        grid=(nb, n_rows // tr),
        in_specs=in_specs,
        out_specs=[pl.BlockSpec((1, tr, c), lambda b, t: (b, t, 0)) for c in out_cols],
        out_shape=[jax.ShapeDtypeStruct((nb, n_rows, c), BF16) for c in out_cols],
        compiler_params=_params(("arbitrary", "arbitrary")),
        name="inproj_rope" if use_rope else "inproj_ctx",
    )(*args)


def _attn_kernel(*refs, n_src):
    qn_ref, qr_ref = refs[0], refs[1]
    src = refs[2:2 + 3 * n_src]
    o_ref = refs[2 + 3 * n_src]
    k_scr, v_scr = refs[3 + 3 * n_src], refs[4 + 3 * n_src]

    @pl.when(pl.program_id(2) == 0)
    def _():
        off = 0
        for i in range(n_src):
            kn_ref, kr_ref, v_ref = src[3 * i:3 * i + 3]
            n = kn_ref.shape[1]
            lane = lax.broadcasted_iota(jnp.int32, (n, ROPE_TILE), 1)
            keep = (lane // QK_ROPE) == (pl.program_id(1) % 2)
            k_scr[off:off + n, :QK_NOPE] = kn_ref[0]
            k_scr[off:off + n, QK_NOPE:] = jnp.where(keep, kr_ref[0].astype(F32),
                                                     0.0).astype(BF16)
            v_scr[off:off + n, :] = v_ref[0]
            off += n

    q = jnp.concatenate([qn_ref[0], qr_ref[0]], axis=1)
    s = lax.dot_general(q, k_scr[...], (((1,), (1,)), ((), ())), preferred_element_type=F32)
    m = jnp.max(s, axis=1, keepdims=True)
    e = jnp.exp(s - m)
    l = jnp.sum(e, axis=1, keepdims=True)
    o = _dot(e.astype(BF16), v_scr[...])
    o_ref[0] = (o / l).astype(BF16)


def _attention(q, kv_list, tq):
    nb, n_q, _ = q.shape
    nope_blocks = MLA_HEADS * QK_NOPE // LANE
    in_specs = [
        pl.BlockSpec((1, tq, QK_NOPE), lambda b, h, t: (b, t, h)),
        pl.BlockSpec((1, tq, ROPE_TILE), lambda b, h, t: (b, t, nope_blocks + h // 2)),
    ]
    args = [q, q]
    n_keys = 0
    for kv in kv_list:
        n = kv.shape[1]
        n_keys += n
        in_specs += [
            pl.BlockSpec((1, n, QK_NOPE), lambda b, h, t: (b, 0, h)),
            pl.BlockSpec((1, n, ROPE_TILE), lambda b, h, t: (b, 0, nope_blocks)),
            pl.BlockSpec((1, n, V_DIM), lambda b, h, t: (b, 0, nope_blocks + 1 + h)),
        ]
        args += [kv, kv, kv]
    return pl.pallas_call(
        functools.partial(_attn_kernel, n_src=len(kv_list)),
        grid=(nb, MLA_HEADS, n_q // tq),
        in_specs=in_specs,
        out_specs=pl.BlockSpec((1, tq, V_DIM), lambda b, h, t: (b, t, h)),
        out_shape=jax.ShapeDtypeStruct((nb, n_q, MLA_W), BF16),
        scratch_shapes=[pltpu.VMEM((n_keys, QK_NOPE + ROPE_TILE), BF16),
                        pltpu.VMEM((n_keys, V_DIM), BF16)],
        compiler_params=_params(("arbitrary", "arbitrary", "arbitrary")),
        name="attn_%d" % len(kv_list),
    )(*args)


def _merge_kernel(att_ref, rest_ref, x_ref, mod_ref, chan_ref, dft_ref, cw_ref, cb_ref,
                  gn_ref, wout_ref, o_ref, ab_scr, yc_scr, *, seq_len, tr):
    t = pl.program_id(1)

    @pl.when(t == 0)
    def _():
        ab = _dot(rest_ref[0, :, :FNET_W], chan_ref[...])
        ab_scr[0:seq_len, :] = ab[:, :FNET_W].astype(BF16)
        ab_scr[seq_len:2 * seq_len, :] = ab[:, FNET_W:].astype(BF16)
        bg = rest_ref[0, :, FNET_W:FNET_W + CONV_W].astype(F32)
        cg = rest_ref[0, :, FNET_W + CONV_W:FNET_W + 2 * CONV_W].astype(F32)
        xv = rest_ref[0, :, FNET_W + 2 * CONV_W:].astype(F32)
        yc = bg * _conv3_rows(cg * xv, cw_ref, cb_ref, seq_len)
        yc_scr[...] = _rms(yc, gn_ref[:, MLA_W + FNET_W:]).astype(BF16)

    yf = _dot(dft_ref[...], ab_scr[...])
    yf = _rms(yf, gn_ref[:, MLA_W:MLA_W + FNET_W]).astype(BF16)
    ya = _rms(att_ref[0].astype(F32), gn_ref[:, :MLA_W]).astype(BF16)
    r0 = pl.multiple_of(t * tr, tr)
    y = jnp.concatenate([ya, yf, yc_scr[pl.ds(r0, tr), :]], axis=1)
    o_ref[0] = x_ref[0] + mod_ref[0, 2:3, :] * _dot(y, wout_ref[...])


def _merge(att, rest, x, mod, chan_dft, pos_dft, cw, cb, gn, wout, tr):
    nb, seq_len, _ = x.shape
    const = lambda b, t: (0, 0)
    return pl.pallas_call(
        functools.partial(_merge_kernel, seq_len=seq_len, tr=tr),
        grid=(nb, seq_len // tr),
        in_specs=[
            pl.BlockSpec((1, tr, MLA_W), lambda b, t: (b, t, 0)),
            pl.BlockSpec((1, seq_len, REST_COLS), lambda b, t: (b, 0, 0)),
            pl.BlockSpec((1, tr, D_MODEL), lambda b, t: (b, t, 0)),
            pl.BlockSpec((1, 6, D_MODEL), lambda b, t: (b, 0, 0)),
            pl.BlockSpec((FNET_W, 2 * FNET_W), const),
            pl.BlockSpec((tr, 2 * seq_len), lambda b, t: (t, 0)),
            pl.BlockSpec((3, CONV_W), const),
            pl.BlockSpec((1, CONV_W), const),
            pl.BlockSpec((1, D_MIX), const),
            pl.BlockSpec((D_MIX, D_MODEL), const),
        ],
        out_specs=pl.BlockSpec((1, tr, D_MODEL), lambda b, t: (b, t, 0)),
        out_shape=jax.ShapeDtypeStruct(x.shape, F32),
        scratch_shapes=[pltpu.VMEM((2 * seq_len, FNET_W), BF16),
                        pltpu.VMEM((seq_len, CONV_W), BF16)],
        compiler_params=_params(("arbitrary", "arbitrary")),
        name="merge_%d" % seq_len,
    )(att, rest, x, mod, chan_dft, pos_dft, cw, cb, gn, wout)


def _ffn_kernel(*refs, seq_len, final):
    if final:
        (x_ref, mod_ref, g2_ref, wg_ref, wv_ref, cw_ref, cb_ref, wd_ref, fg_ref,
         o_ref, h_scr) = refs
    else:
        (x_ref, mod_ref, g2_ref, wg_ref, wv_ref, cw_ref, cb_ref, wd_ref,
         o_ref, h_scr) = refs
    j = pl.program_id(1)

    @pl.when(j == 0)
    def _():
        h = _rms(x_ref[0], g2_ref[...]) * (1.0 + mod_ref[0, 4:5, :]) + mod_ref[0, 3:4, :]
        h_scr[...] = h.astype(BF16)

    h = h_scr[...]
    gate = _conv3_rows(_dot(h, wg_ref[...]), cw_ref, cb_ref, seq_len)
    act = (gate * jax.nn.sigmoid(gate) * _dot(h, wv_ref[...])).astype(BF16)
    part = _dot(act, wd_ref[...])

    @pl.when(j == 0)
    def _():
        o_ref[0] = part

    @pl.when(j > 0)
    def _():
        o_ref[0] += part

    @pl.when(j == pl.num_programs(1) - 1)
    def _():
        y = x_ref[0] + mod_ref[0, 5:6, :] * o_ref[0]
        if final:
            y = _rms(y, fg_ref[...])
        o_ref[0] = y


def _ffn(x, mod, g2, wup, cw, cb, wdown, seq_len, final_g, fc):
    nb, n_rows, _ = x.shape
    n_chunk = D_FF // fc
    final = final_g is not None
    const = lambda b, j: (0, 0)
    in_specs = [
        pl.BlockSpec((1, n_rows, D_MODEL), lambda b, j: (b, 0, 0)),
        pl.BlockSpec((1, 6, D_MODEL), lambda b, j: (b, 0, 0)),
        pl.BlockSpec((1, D_MODEL), const),
        pl.BlockSpec((D_MODEL, fc), lambda b, j: (0, j)),
        pl.BlockSpec((D_MODEL, fc), lambda b, j: (0, n_chunk + j)),
        pl.BlockSpec((3, fc), lambda b, j: (0, j)),
        pl.BlockSpec((1, fc), lambda b, j: (0, j)),
        pl.BlockSpec((fc, D_MODEL), lambda b, j: (j, 0)),
    ]
    args = [x, mod, g2, wup, wup, cw, cb, wdown]
    if final:
        in_specs.append(pl.BlockSpec((1, D_MODEL), const))
        args.append(final_g)
    return pl.pallas_call(
        functools.partial(_ffn_kernel, seq_len=seq_len, final=final),
        grid=(nb, n_chunk),
        in_specs=in_specs,
        out_specs=pl.BlockSpec((1, n_rows, D_MODEL), lambda b, j: (b, 0, 0)),
        out_shape=jax.ShapeDtypeStruct(x.shape, F32),
        scratch_shapes=[pltpu.VMEM((n_rows, D_MODEL), BF16)],
        compiler_params=_params(("arbitrary", "arbitrary")),
        name="ffn_final" if final else "ffn_%d" % seq_len,
    )(*args)


def _rope_perm():
    quarter = QK_ROPE // 4
    idx = [a * 2 * quarter + j * quarter + i
           for j in range(2) for a in range(2) for i in range(quarter)]
    return np.asarray(idx, np.int32)


@functools.lru_cache(maxsize=None)
def _rope_table(n_tokens):
    t = np.arange(n_tokens)
    half = QK_ROPE // 2
    inv = ROPE_THETA ** (-np.arange(0, half, 2, dtype=np.float64) / half)
    ang = np.concatenate([(t // GRID_W)[:, None] * inv, (t % GRID_W)[:, None] * inv], axis=1)
    cos, sin, zero = np.cos(ang), np.sin(ang), np.zeros_like(ang)
    per_head = lambda a, b: np.tile(np.concatenate([a, b], axis=1), (1, ROPE_TILE // QK_ROPE))
    tab = np.concatenate([per_head(cos, cos), per_head(zero, sin), per_head(-sin, zero)], axis=1)
    return tab.astype(np.float32)


@functools.lru_cache(maxsize=None)
def _chan_dft():
    gw = FNET_W // FNET_GROUPS
    ang = 2.0 * np.pi * ((np.arange(gw)[:, None] * np.arange(gw)[None, :]) % gw) / gw
    eye = np.eye(FNET_GROUPS)
    return np.concatenate([np.kron(eye, np.cos(ang)), np.kron(eye, np.sin(ang))],
                          axis=1) * gw ** -0.5


@functools.lru_cache(maxsize=None)
def _pos_dft(seq_len):
    ang = 2.0 * np.pi * ((np.arange(seq_len)[:, None] * np.arange(seq_len)[None, :])
                         % seq_len) / seq_len
    return np.concatenate([np.cos(ang), -np.sin(ang)], axis=1) * seq_len ** -0.5


def kernel(x, c, ctx, c_ctx, ada_w, ada_b, norm1_g, w_in, q_norm_g, kv_norm_g, w_uq, w_ukv,
           sconv_w, sconv_b, out_norm_g, w_out, norm2_g, w_up, ffconv_w, ffconv_b, w_down,
           final_g):
    nb, seq, _ = x.shape
    n_ctx = ctx.shape[1]

    cond = jnp.concatenate([c, c_ctx[None, :]], axis=0)
    cond = jnp.pad(cond, ((0, MOD_ROWS - cond.shape[0]), (0, 0)))
    mod = _ada_mod(cond, ada_w, ada_b)

    perm = _rope_perm()
    rope_tab = jnp.asarray(_rope_table(seq))
    chan_dft = jnp.asarray(_chan_dft(), F32).astype(BF16)
    pos_dft_lat = jnp.asarray(_pos_dft(seq), F32).astype(BF16)
    pos_dft_ctx = jnp.asarray(_pos_dft(n_ctx), F32).astype(BF16)

    x_lat = x
    x_ctx = ctx.reshape(1, nb * n_ctx, D_MODEL)
    row = lambda v: v.reshape(1, -1)
    for l in range(DEPTH):
        last = l == DEPTH - 1
        mod_l = mod[l, :nb].reshape(nb, 6, D_MODEL)
        mod_c = mod[l, nb:nb + 1].reshape(1, 6, D_MODEL)

        k_r = w_in[l][:, Q_LORA + KV_LORA:Q_LORA + KV_LORA + QK_ROPE][:, perm]
        win = jnp.concatenate([w_in[l][:, :Q_LORA + KV_LORA], k_r, k_r,
                               w_in[l][:, Q_LORA + KV_LORA + QK_ROPE:]], axis=1).astype(BF16)
        wq = jnp.concatenate([w_uq[l][:, :, :QK_NOPE].reshape(Q_LORA, -1),
                              w_uq[l][:, :, QK_NOPE:][:, :, perm].reshape(Q_LORA, -1)],
                             axis=1).astype(BF16)
        wkv = jnp.concatenate([w_ukv[l][:, :, :QK_NOPE].reshape(KV_LORA, -1),
                               w_ukv[l][:, :, QK_NOPE:].reshape(KV_LORA, -1)],
                              axis=1).astype(BF16)
        wout = w_out[l].astype(BF16)
        wup = w_up[l].astype(BF16)
        wdown = w_down[l].astype(BF16)
        proj = (row(norm1_g[l]), win, row(q_norm_g[l]), row(kv_norm_g[l]), wq, wkv)
        mix = (sconv_w[l], row(sconv_b[l]), row(out_norm_g[l]), wout)
        ffn = (row(norm2_g[l]), wup, ffconv_w[l], row(ffconv_b[l]), wdown)

        q_l, kv_l, rest_l = _inproj(x_lat, mod_l, *proj, rope_tab, tr=512)
        q_c, kv_c, rest_c = _inproj(x_ctx, mod_c, *proj, None, tr=512)
        q_c = q_c.reshape(nb, n_ctx, Q_COLS)
        kv_c = kv_c.reshape(nb, n_ctx, KV_COLS)

        att_l = _attention(q_l, [kv_c, kv_l], tq=512)
        x_lat = _merge(att_l, rest_l, x_lat, mod_l, chan_dft, pos_dft_lat, *mix, tr=512)
        x_lat = _ffn(x_lat, mod_l, *ffn, seq_len=seq,
                     final_g=row(final_g) if last else None, fc=256)
        if not last:
            att_c = _attention(q_c, [kv_c], tq=n_ctx)
            x_ctx = _merge(att_c, rest_c.reshape(nb, n_ctx, REST_COLS),
                           x_ctx.reshape(nb, n_ctx, D_MODEL),
                           jnp.broadcast_to(mod_c, (nb, 6, D_MODEL)),
                           chan_dft, pos_dft_ctx, *mix, tr=n_ctx)
            x_ctx = _ffn(x_ctx.reshape(1, nb * n_ctx, D_MODEL), mod_c, *ffn,
                         seq_len=n_ctx, final_g=None, fc=256)
    return x_lat
```

```python
import functools

import jax
import jax.numpy as jnp
import numpy as np
from jax import lax
from jax.experimental import pallas as pl
from jax.experimental.pallas import tpu as pltpu

D_MODEL = 1024
DEPTH = 2
GRID_W = 64
MLA_HEADS = 4
QK_NOPE = 128
QK_ROPE = 64
V_DIM = 128
Q_LORA = 384
KV_LORA = 256
MLA_W = MLA_HEADS * V_DIM
FNET_GROUPS = 4
FNET_W = 256
CONV_W = 256
D_MIX = MLA_W + FNET_W + CONV_W
D_FF = 2816
ROPE_THETA = 10000.0
EPS = 1e-6
SM_SCALE = (QK_NOPE + QK_ROPE) ** -0.5
Q_SCALE = SM_SCALE * float(np.log2(np.e))

LANE = 128
SUBLANE = 8
ROPE_TILE = 2 * QK_ROPE
P_COLS = Q_LORA + KV_LORA + ROPE_TILE + FNET_W + 3 * CONV_W
Q_COLS = MLA_HEADS * QK_NOPE + MLA_HEADS * QK_ROPE
KV_COLS = MLA_HEADS * QK_NOPE + ROPE_TILE + MLA_HEADS * V_DIM
REST_COLS = FNET_W + 3 * CONV_W
CHAIN_ROWS = 256
FFN_GROUP_ROWS = 512
MOD_ROWS = 16
VMEM_LIMIT = 56 * 1024 * 1024

BF16 = jnp.bfloat16
F32 = jnp.float32


def _params(sem):
    return pltpu.CompilerParams(dimension_semantics=sem, vmem_limit_bytes=VMEM_LIMIT)


def _rms(x, g):
    return x * lax.rsqrt(jnp.mean(x * x, axis=-1, keepdims=True) + EPS) * g


def _dot(a, b):
    return jnp.dot(a, b, preferred_element_type=F32)


def _conv3_rows(z, w_ref, b_ref, seq_len, row0=0):
    n = z.shape[0]
    row = (lax.broadcasted_iota(jnp.int32, z.shape, 0) + row0) % seq_len
    prev = jnp.where(row == 0, 0.0, pltpu.roll(z, 1, axis=0))
    nxt = jnp.where(row == seq_len - 1, 0.0, pltpu.roll(z, n - 1, axis=0))
    return prev * w_ref[0:1, :] + z * w_ref[1:2, :] + nxt * w_ref[2:3, :] + b_ref[...]


def _ada_kernel(c_ref, w_ref, b_ref, o_ref):
    c = c_ref[...]
    s = (c * jax.nn.sigmoid(c)).astype(BF16)
    o_ref[0] = _dot(s, w_ref[0].astype(BF16)) + b_ref[0]


def _ada_mod(cond, ada_w, ada_b):
    n_col = 6 * D_MODEL
    bc = 1536
    return pl.pallas_call(
        _ada_kernel,
        grid=(DEPTH, n_col // bc),
        in_specs=[
            pl.BlockSpec((MOD_ROWS, D_MODEL), lambda l, j: (0, 0)),
            pl.BlockSpec((1, D_MODEL, bc), lambda l, j: (l, 0, j)),
            pl.BlockSpec((1, 1, bc), lambda l, j: (l, 0, j)),
        ],
        out_specs=pl.BlockSpec((1, MOD_ROWS, bc), lambda l, j: (l, 0, j)),
        out_shape=jax.ShapeDtypeStruct((DEPTH, MOD_ROWS, n_col), F32),
        compiler_params=_params(("arbitrary", "arbitrary")),
        name="ada_mod",
    )(cond, ada_w, ada_b.reshape(DEPTH, 1, n_col))


def _rope_tile(t, tab):
    return (t * tab[:, 0:LANE]
            + pltpu.roll(t, QK_ROPE // 2, axis=1) * tab[:, LANE:2 * LANE]
            + pltpu.roll(t, LANE - QK_ROPE // 2, axis=1) * tab[:, 2 * LANE:3 * LANE])


def _inproj_kernel(*refs, use_rope):
    if use_rope:
        (x_ref, mod_ref, g1_ref, win_ref, qg_ref, kvg_ref, wq_ref, wkv_ref, tab_ref,
         q_ref, kv_ref, rest_ref) = refs
    else:
        (x_ref, mod_ref, g1_ref, win_ref, qg_ref, kvg_ref, wq_ref, wkv_ref,
         q_ref, kv_ref, rest_ref) = refs
    n_nope = MLA_HEADS * QK_NOPE
    tr = x_ref.shape[1]
    sub = min(tr, CHAIN_ROWS)
    for c in range(tr // sub):
        rows = slice(c * sub, (c + 1) * sub)
        x = x_ref[0, rows, :]
        h = _rms(x, g1_ref[...]) * (1.0 + mod_ref[0, 1:2, :]) + mod_ref[0, 0:1, :]
        p = _dot(h.astype(BF16), win_ref[...])
        cq = _rms(p[:, :Q_LORA], qg_ref[...]).astype(BF16)
        ckv = _rms(p[:, Q_LORA:Q_LORA + KV_LORA], kvg_ref[...]).astype(BF16)
        q = _dot(cq, wq_ref[...])
        kv = _dot(ckv, wkv_ref[...])
        kr = p[:, Q_LORA + KV_LORA:Q_LORA + KV_LORA + ROPE_TILE]
        qr = [q[:, n_nope + i * LANE:n_nope + (i + 1) * LANE] for i in range(2)]
        if use_rope:
            tab = tab_ref[rows, :]
            kr = _rope_tile(kr, tab)
            qr = [_rope_tile(t, tab) for t in qr]
        q_ref[0, rows, :n_nope] = (q[:, :n_nope] * Q_SCALE).astype(BF16)
        for i in range(2):
            q_ref[0, rows, n_nope + i * LANE:n_nope + (i + 1) * LANE] = (
                qr[i] * Q_SCALE).astype(BF16)
        kv_ref[0, rows, :n_nope] = kv[:, :n_nope].astype(BF16)
        kv_ref[0, rows, n_nope:n_nope + ROPE_TILE] = kr.astype(BF16)
        kv_ref[0, rows, n_nope + ROPE_TILE:] = kv[:, n_nope:].astype(BF16)
        rest_ref[0, rows, :] = p[:, Q_LORA + KV_LORA + ROPE_TILE:].astype(BF16)


def _inproj(x, mod, g1, win, qg, kvg, wq, wkv, rope_tab, tr):
    nb, n_rows, _ = x.shape
    use_rope = rope_tab is not None
    const = lambda b, t: (0, 0)
    in_specs = [
        pl.BlockSpec((1, tr, D_MODEL), lambda b, t: (b, t, 0)),
        pl.BlockSpec((1, 6, D_MODEL), lambda b, t: (b, 0, 0)),
        pl.BlockSpec((1, D_MODEL), const),
        pl.BlockSpec((D_MODEL, P_COLS), const),
        pl.BlockSpec((1, Q_LORA), const),
        pl.BlockSpec((1, KV_LORA), const),
        pl.BlockSpec((Q_LORA, Q_COLS), const),
        pl.BlockSpec((KV_LORA, 2 * MLA_W), const),
    ]
    args = [x, mod, g1, win, qg, kvg, wq, wkv]
    if use_rope:
        in_specs.append(pl.BlockSpec((tr, 3 * LANE), lambda b, t: (t, 0)))
        args.append(rope_tab)
    out_cols = (Q_COLS, KV_COLS, REST_COLS)
    return pl.pallas_call(
        functools.partial(_inproj_kernel, use_rope=use_rope),
        grid=(nb, n_rows // tr),
        in_specs=in_specs,
        out_specs=[pl.BlockSpec((1, tr, c), lambda b, t: (b, t, 0)) for c in out_cols],
        out_shape=[jax.ShapeDtypeStruct((nb, n_rows, c), BF16) for c in out_cols],
        compiler_params=_params(("arbitrary", "arbitrary")),
        name="inproj_rope" if use_rope else "inproj_ctx",
    )(*args)


def _attn_kernel(*refs, n_src, tq):
    qn_ref, qr_ref = refs[0], refs[1]
    src = refs[2:2 + 3 * n_src]
    o_ref = refs[2 + 3 * n_src]
    k_scr, v_scr = refs[3 + 3 * n_src], refs[4 + 3 * n_src]

    off = 0
    for i in range(n_src):
        kn_ref, kr_ref, v_ref = src[3 * i:3 * i + 3]
        n = kn_ref.shape[1]
        lane = lax.broadcasted_iota(jnp.int32, (n, ROPE_TILE), 1)
        keep = (lane // QK_ROPE) == (pl.program_id(1) % 2)
        k_scr[off:off + n, :QK_NOPE] = kn_ref[0]
        k_scr[off:off + n, QK_NOPE:] = jnp.where(keep, kr_ref[0].astype(F32),
                                                 0.0).astype(BF16)
        v_scr[off:off + n, :] = v_ref[0]
        off += n

    for t in range(qn_ref.shape[1] // tq):
        rows = slice(t * tq, (t + 1) * tq)
        q = jnp.concatenate([qn_ref[0, rows, :], qr_ref[0, rows, :]], axis=1)
        s = lax.dot_general(q, k_scr[...], (((1,), (1,)), ((), ())),
                            preferred_element_type=F32)
        e = jnp.exp2(s - jnp.max(s, axis=1, keepdims=True))
        l = jnp.sum(e, axis=1, keepdims=True)
        o = _dot(e.astype(BF16), v_scr[...])
        o_ref[0, rows, :] = (o / l).astype(BF16)


def _attention(q, kv_list, tq):
    nb, n_q, _ = q.shape
    nope_blocks = MLA_HEADS * QK_NOPE // LANE
    in_specs = [
        pl.BlockSpec((1, n_q, QK_NOPE), lambda b, h: (b, 0, h)),
        pl.BlockSpec((1, n_q, ROPE_TILE), lambda b, h: (b, 0, nope_blocks + h // 2)),
    ]
    args = [q, q]
    n_keys = 0
    for kv in kv_list:
        n = kv.shape[1]
        n_keys += n
        in_specs += [
            pl.BlockSpec((1, n, QK_NOPE), lambda b, h: (b, 0, h)),
            pl.BlockSpec((1, n, ROPE_TILE), lambda b, h: (b, 0, nope_blocks)),
            pl.BlockSpec((1, n, V_DIM), lambda b, h: (b, 0, nope_blocks + 1 + h)),
        ]
        args += [kv, kv, kv]
    return pl.pallas_call(
        functools.partial(_attn_kernel, n_src=len(kv_list), tq=tq),
        grid=(nb, MLA_HEADS),
        in_specs=in_specs,
        out_specs=pl.BlockSpec((1, n_q, V_DIM), lambda b, h: (b, 0, h)),
        out_shape=jax.ShapeDtypeStruct((nb, n_q, MLA_W), BF16),
        scratch_shapes=[pltpu.VMEM((n_keys, QK_NOPE + ROPE_TILE), BF16),
                        pltpu.VMEM((n_keys, V_DIM), BF16)],
        compiler_params=_params(("arbitrary", "arbitrary")),
        name="attn_%d" % len(kv_list),
    )(*args)


def _merge_kernel(att_ref, rest_ref, x_ref, mod_ref, chan_ref, dft_ref, cw_ref, cb_ref,
                  gn_ref, wout_ref, o_ref, ab_scr, yc_scr, *, seq_len, tr):
    t = pl.program_id(1)

    @pl.when(t == 0)
    def _():
        ab = _dot(rest_ref[0, :, :FNET_W], chan_ref[...])
        ab_scr[0:seq_len, :] = ab[:, :FNET_W].astype(BF16)
        ab_scr[seq_len:2 * seq_len, :] = ab[:, FNET_W:].astype(BF16)
        bg = rest_ref[0, :, FNET_W:FNET_W + CONV_W].astype(F32)
        cg = rest_ref[0, :, FNET_W + CONV_W:FNET_W + 2 * CONV_W].astype(F32)
        xv = rest_ref[0, :, FNET_W + 2 * CONV_W:].astype(F32)
        yc = bg * _conv3_rows(cg * xv, cw_ref, cb_ref, seq_len)
        yc_scr[...] = _rms(yc, gn_ref[:, MLA_W + FNET_W:]).astype(BF16)

    sub = min(tr, CHAIN_ROWS)
    for c in range(tr // sub):
        rows = slice(c * sub, (c + 1) * sub)
        yf = _dot(dft_ref[rows, :], ab_scr[...])
        yf = _rms(yf, gn_ref[:, MLA_W:MLA_W + FNET_W]).astype(BF16)
        ya = _rms(att_ref[0, rows, :].astype(F32), gn_ref[:, :MLA_W]).astype(BF16)
        r0 = pl.multiple_of(t * tr + c * sub, sub)
        y = jnp.concatenate([ya, yf, yc_scr[pl.ds(r0, sub), :]], axis=1)
        o_ref[0, rows, :] = x_ref[0, rows, :] + mod_ref[0, 2:3, :] * _dot(y, wout_ref[...])


def _merge(att, rest, x, mod, chan_dft, pos_dft, cw, cb, gn, wout, tr):
    nb, seq_len, _ = x.shape
    const = lambda b, t: (0, 0)
    return pl.pallas_call(
        functools.partial(_merge_kernel, seq_len=seq_len, tr=tr),
        grid=(nb, seq_len // tr),
        in_specs=[
            pl.BlockSpec((1, tr, MLA_W), lambda b, t: (b, t, 0)),
            pl.BlockSpec((1, seq_len, REST_COLS), lambda b, t: (b, 0, 0)),
            pl.BlockSpec((1, tr, D_MODEL), lambda b, t: (b, t, 0)),
            pl.BlockSpec((1, 6, D_MODEL), lambda b, t: (b, 0, 0)),
            pl.BlockSpec((FNET_W, 2 * FNET_W), const),
            pl.BlockSpec((tr, 2 * seq_len), lambda b, t: (t, 0)),
            pl.BlockSpec((3, CONV_W), const),
            pl.BlockSpec((1, CONV_W), const),
            pl.BlockSpec((1, D_MIX), const),
            pl.BlockSpec((D_MIX, D_MODEL), const),
        ],
        out_specs=pl.BlockSpec((1, tr, D_MODEL), lambda b, t: (b, t, 0)),
        out_shape=jax.ShapeDtypeStruct(x.shape, F32),
        scratch_shapes=[pltpu.VMEM((2 * seq_len, FNET_W), BF16),
                        pltpu.VMEM((seq_len, CONV_W), BF16)],
        compiler_params=_params(("arbitrary", "arbitrary")),
        name="merge_%d" % seq_len,
    )(att, rest, x, mod, chan_dft, pos_dft, cw, cb, gn, wout)


def _ffn_kernel(*refs, seq_len, final, group_rows):
    if final:
        (x_ref, mod_ref, g2_ref, wg_ref, wv_ref, cw_ref, cb_ref, wd_ref, fg_ref,
         o_ref, h_scr) = refs
    else:
        (x_ref, mod_ref, g2_ref, wg_ref, wv_ref, cw_ref, cb_ref, wd_ref,
         o_ref, h_scr) = refs
    j = pl.program_id(1)
    n_rows = x_ref.shape[1]

    @pl.when(j == 0)
    def _():
        h = _rms(x_ref[0], g2_ref[...]) * (1.0 + mod_ref[0, 4:5, :]) + mod_ref[0, 3:4, :]
        h_scr[...] = h.astype(BF16)
        o_ref[0] = jnp.zeros(o_ref.shape[1:], F32)

    for r0 in range(0, n_rows, group_rows):
        r1 = r0 + group_rows
        e0, e1 = max(r0 - SUBLANE, 0), min(r1 + SUBLANE, n_rows)
        h = h_scr[e0:e1, :]
        gate = _conv3_rows(_dot(h, wg_ref[...]), cw_ref, cb_ref, seq_len, row0=e0)
        act = gate * jax.nn.sigmoid(gate) * _dot(h, wv_ref[...])
        o_ref[0, r0:r1, :] += _dot(act[r0 - e0:r1 - e0].astype(BF16), wd_ref[...])

    @pl.when(j == pl.num_programs(1) - 1)
    def _():
        y = x_ref[0] + mod_ref[0, 5:6, :] * o_ref[0]
        if final:
            y = _rms(y, fg_ref[...])
        o_ref[0] = y


def _ffn(x, mod, g2, wup, cw, cb, wdown, seq_len, final_g, fc):
    nb, n_rows, _ = x.shape
    n_chunk = D_FF // fc
    final = final_g is not None
    const = lambda b, j: (0, 0)
    in_specs = [
        pl.BlockSpec((1, n_rows, D_MODEL), lambda b, j: (b, 0, 0)),
        pl.BlockSpec((1, 6, D_MODEL), lambda b, j: (b, 0, 0)),
        pl.BlockSpec((1, D_MODEL), const),
        pl.BlockSpec((D_MODEL, fc), lambda b, j: (0, j)),
        pl.BlockSpec((D_MODEL, fc), lambda b, j: (0, n_chunk + j)),
        pl.BlockSpec((3, fc), lambda b, j: (0, j)),
        pl.BlockSpec((1, fc), lambda b, j: (0, j)),
        pl.BlockSpec((fc, D_MODEL), lambda b, j: (j, 0)),
    ]
    args = [x, mod, g2, wup, wup, cw, cb, wdown]
    if final:
        in_specs.append(pl.BlockSpec((1, D_MODEL), const))
        args.append(final_g)
    return pl.pallas_call(
        functools.partial(_ffn_kernel, seq_len=seq_len, final=final, group_rows=FFN_GROUP_ROWS),
        grid=(nb, n_chunk),
        in_specs=in_specs,
        out_specs=pl.BlockSpec((1, n_rows, D_MODEL), lambda b, j: (b, 0, 0)),
        out_shape=jax.ShapeDtypeStruct(x.shape, F32),
        scratch_shapes=[pltpu.VMEM((n_rows, D_MODEL), BF16)],
        compiler_params=_params(("arbitrary", "arbitrary")),
        name="ffn_final" if final else "ffn_%d" % seq_len,
    )(*args)


def _rope_perm():
    quarter = QK_ROPE // 4
    idx = [a * 2 * quarter + j * quarter + i
           for j in range(2) for a in range(2) for i in range(quarter)]
    return np.asarray(idx, np.int32)


@functools.lru_cache(maxsize=None)
def _rope_table(n_tokens):
    t = np.arange(n_tokens)
    half = QK_ROPE // 2
    inv = ROPE_THETA ** (-np.arange(0, half, 2, dtype=np.float64) / half)
    ang = np.concatenate([(t // GRID_W)[:, None] * inv, (t % GRID_W)[:, None] * inv], axis=1)
    cos, sin, zero = np.cos(ang), np.sin(ang), np.zeros_like(ang)
    per_head = lambda a, b: np.tile(np.concatenate([a, b], axis=1), (1, ROPE_TILE // QK_ROPE))
    tab = np.concatenate([per_head(cos, cos), per_head(zero, sin), per_head(-sin, zero)], axis=1)
    return tab.astype(np.float32)


@functools.lru_cache(maxsize=None)
def _chan_dft():
    gw = FNET_W // FNET_GROUPS
    ang = 2.0 * np.pi * ((np.arange(gw)[:, None] * np.arange(gw)[None, :]) % gw) / gw
    eye = np.eye(FNET_GROUPS)
    return np.concatenate([np.kron(eye, np.cos(ang)), np.kron(eye, np.sin(ang))],
                          axis=1) * gw ** -0.5


@functools.lru_cache(maxsize=None)
def _pos_dft(seq_len):
    ang = 2.0 * np.pi * ((np.arange(seq_len)[:, None] * np.arange(seq_len)[None, :])
                         % seq_len) / seq_len
    return np.concatenate([np.cos(ang), -np.sin(ang)], axis=1) * seq_len ** -0.5


def kernel(x, c, ctx, c_ctx, ada_w, ada_b, norm1_g, w_in, q_norm_g, kv_norm_g, w_uq, w_ukv,
           sconv_w, sconv_b, out_norm_g, w_out, norm2_g, w_up, ffconv_w, ffconv_b, w_down,
           final_g):
    nb, seq, _ = x.shape
    n_ctx = ctx.shape[1]

    cond = jnp.concatenate([c, c_ctx[None, :]], axis=0)
    cond = jnp.pad(cond, ((0, MOD_ROWS - cond.shape[0]), (0, 0)))
    mod = _ada_mod(cond, ada_w, ada_b)

    perm = _rope_perm()
    rope_tab = jnp.asarray(_rope_table(seq))
    chan_dft = jnp.asarray(_chan_dft(), F32).astype(BF16)
    pos_dft_lat = jnp.asarray(_pos_dft(seq), F32).astype(BF16)
    pos_dft_ctx = jnp.asarray(_pos_dft(n_ctx), F32).astype(BF16)

    x_lat = x
    x_ctx = ctx.reshape(1, nb * n_ctx, D_MODEL)
    row = lambda v: v.reshape(1, -1)
    for l in range(DEPTH):
        last = l == DEPTH - 1
        mod_l = mod[l, :nb].reshape(nb, 6, D_MODEL)
        mod_c = mod[l, nb:nb + 1].reshape(1, 6, D_MODEL)

        k_r = w_in[l][:, Q_LORA + KV_LORA:Q_LORA + KV_LORA + QK_ROPE][:, perm]
        win = jnp.concatenate([w_in[l][:, :Q_LORA + KV_LORA], k_r, k_r,
                               w_in[l][:, Q_LORA + KV_LORA + QK_ROPE:]], axis=1).astype(BF16)
        wq = jnp.concatenate([w_uq[l][:, :, :QK_NOPE].reshape(Q_LORA, -1),
                              w_uq[l][:, :, QK_NOPE:][:, :, perm].reshape(Q_LORA, -1)],
                             axis=1).astype(BF16)
        wkv = jnp.concatenate([w_ukv[l][:, :, :QK_NOPE].reshape(KV_LORA, -1),
                               w_ukv[l][:, :, QK_NOPE:].reshape(KV_LORA, -1)],
                              axis=1).astype(BF16)
        wout = w_out[l].astype(BF16)
        wup = w_up[l].astype(BF16)
        wdown = w_down[l].astype(BF16)
        proj = (row(norm1_g[l]), win, row(q_norm_g[l]), row(kv_norm_g[l]), wq, wkv)
        mix = (sconv_w[l], row(sconv_b[l]), row(out_norm_g[l]), wout)
        ffn = (row(norm2_g[l]), wup, ffconv_w[l], row(ffconv_b[l]), wdown)

        q_l, kv_l, rest_l = _inproj(x_lat, mod_l, *proj, rope_tab, tr=512)
        q_c, kv_c, rest_c = _inproj(x_ctx, mod_c, *proj, None, tr=512)
        q_c = q_c.reshape(nb, n_ctx, Q_COLS)
        kv_c = kv_c.reshape(nb, n_ctx, KV_COLS)

        att_l = _attention(q_l, [kv_c, kv_l], tq=256)
        x_lat = _merge(att_l, rest_l, x_lat, mod_l, chan_dft, pos_dft_lat, *mix, tr=512)
        x_lat = _ffn(x_lat, mod_l, *ffn, seq_len=seq,
                     final_g=row(final_g) if last else None, fc=256)
        if not last:
            att_c = _attention(q_c, [kv_c], tq=n_ctx)
            x_ctx = _merge(att_c, rest_c.reshape(nb, n_ctx, REST_COLS),
                           x_ctx.reshape(nb, n_ctx, D_MODEL),
                           jnp.broadcast_to(mod_c, (nb, 6, D_MODEL)),
                           chan_dft, pos_dft_ctx, *mix, tr=n_ctx)
            x_ctx = _ffn(x_ctx.reshape(1, nb * n_ctx, D_MODEL), mod_c, *ffn,
                         seq_len=n_ctx, final_g=None, fc=256)
    return x_lat
```

```python
import functools

import jax
import jax.numpy as jnp
import numpy as np
from jax import lax
from jax.experimental import pallas as pl
from jax.experimental.pallas import tpu as pltpu

D_MODEL = 1024
DEPTH = 2
GRID_W = 64
MLA_HEADS = 4
QK_NOPE = 128
QK_ROPE = 64
V_DIM = 128
Q_LORA = 384
KV_LORA = 256
MLA_W = MLA_HEADS * V_DIM
FNET_GROUPS = 4
FNET_W = 256
CONV_W = 256
D_MIX = MLA_W + FNET_W + CONV_W
D_FF = 2816
ROPE_THETA = 10000.0
EPS = 1e-6
SM_SCALE = (QK_NOPE + QK_ROPE) ** -0.5
Q_SCALE = SM_SCALE * float(np.log2(np.e))

LANE = 128
SUBLANE = 8
ROPE_TILE = 2 * QK_ROPE
P_COLS = Q_LORA + KV_LORA + ROPE_TILE + FNET_W + 3 * CONV_W
Q_COLS = MLA_HEADS * QK_NOPE + MLA_HEADS * QK_ROPE
SLAB_QN, SLAB_QR = 0, MLA_HEADS
SLAB_KN = SLAB_QR + MLA_HEADS * QK_ROPE // ROPE_TILE
SLAB_KR = SLAB_KN + MLA_HEADS
SLAB_V = SLAB_KR + 1
N_SLABS = SLAB_V + MLA_HEADS
REST_COLS = FNET_W + 3 * CONV_W
CHAIN_ROWS = 256
FFN_GROUP_ROWS = 512
MOD_ROWS = 16
VMEM_LIMIT = 56 * 1024 * 1024

BF16 = jnp.bfloat16
F32 = jnp.float32


def _params(sem):
    return pltpu.CompilerParams(dimension_semantics=sem, vmem_limit_bytes=VMEM_LIMIT)


def _rms(x, g):
    return x * lax.rsqrt(jnp.mean(x * x, axis=-1, keepdims=True) + EPS) * g


def _dot(a, b):
    return jnp.dot(a, b, preferred_element_type=F32)


def _conv3_rows(z, w_ref, b_ref, seq_len, row0=0):
    n = z.shape[0]
    row = (lax.broadcasted_iota(jnp.int32, z.shape, 0) + row0) % seq_len
    prev = jnp.where(row == 0, 0.0, pltpu.roll(z, 1, axis=0))
    nxt = jnp.where(row == seq_len - 1, 0.0, pltpu.roll(z, n - 1, axis=0))
    return prev * w_ref[0:1, :] + z * w_ref[1:2, :] + nxt * w_ref[2:3, :] + b_ref[...]


def _ada_kernel(c_ref, w_ref, b_ref, o_ref):
    c = c_ref[...]
    s = (c * jax.nn.sigmoid(c)).astype(BF16)
    o_ref[0] = _dot(s, w_ref[0].astype(BF16)) + b_ref[0]


def _ada_mod(cond, ada_w, ada_b):
    n_col = 6 * D_MODEL
    bc = 1536
    return pl.pallas_call(
        _ada_kernel,
        grid=(DEPTH, n_col // bc),
        in_specs=[
            pl.BlockSpec((MOD_ROWS, D_MODEL), lambda l, j: (0, 0)),
            pl.BlockSpec((1, D_MODEL, bc), lambda l, j: (l, 0, j)),
            pl.BlockSpec((1, 1, bc), lambda l, j: (l, 0, j)),
        ],
        out_specs=pl.BlockSpec((1, MOD_ROWS, bc), lambda l, j: (l, 0, j)),
        out_shape=jax.ShapeDtypeStruct((DEPTH, MOD_ROWS, n_col), F32),
        compiler_params=_params(("arbitrary", "arbitrary")),
        name="ada_mod",
    )(cond, ada_w, ada_b.reshape(DEPTH, 1, n_col))


def _rope_tile(t, tab):
    return (t * tab[:, 0:LANE]
            + pltpu.roll(t, QK_ROPE // 2, axis=1) * tab[:, LANE:2 * LANE]
            + pltpu.roll(t, LANE - QK_ROPE // 2, axis=1) * tab[:, 2 * LANE:3 * LANE])


def _inproj_kernel(*refs, use_rope):
    if use_rope:
        (x_ref, mod_ref, g1_ref, win_ref, qg_ref, kvg_ref, wq_ref, wkv_ref, tab_ref,
         heads_ref, rest_ref) = refs
    else:
        (x_ref, mod_ref, g1_ref, win_ref, qg_ref, kvg_ref, wq_ref, wkv_ref,
         heads_ref, rest_ref) = refs
    n_nope = MLA_HEADS * QK_NOPE
    tr = x_ref.shape[1]
    sub = min(tr, CHAIN_ROWS)
    for c in range(tr // sub):
        rows = slice(c * sub, (c + 1) * sub)
        x = x_ref[0, rows, :]
        h = _rms(x, g1_ref[...]) * (1.0 + mod_ref[0, 1:2, :]) + mod_ref[0, 0:1, :]
        p = _dot(h.astype(BF16), win_ref[...])
        cq = _rms(p[:, :Q_LORA], qg_ref[...]).astype(BF16)
        ckv = _rms(p[:, Q_LORA:Q_LORA + KV_LORA], kvg_ref[...]).astype(BF16)
        q = _dot(cq, wq_ref[...])
        kv = _dot(ckv, wkv_ref[...])
        kr = p[:, Q_LORA + KV_LORA:Q_LORA + KV_LORA + ROPE_TILE]
        qr = [q[:, n_nope + i * LANE:n_nope + (i + 1) * LANE] for i in range(2)]
        if use_rope:
            tab = tab_ref[rows, :]
            kr = _rope_tile(kr, tab)
            qr = [_rope_tile(t, tab) for t in qr]
        slabs = ([q[:, i * LANE:(i + 1) * LANE] * Q_SCALE for i in range(MLA_HEADS)]
                 + [t * Q_SCALE for t in qr]
                 + [kv[:, i * LANE:(i + 1) * LANE] for i in range(MLA_HEADS)]
                 + [kr]
                 + [kv[:, n_nope + i * LANE:n_nope + (i + 1) * LANE] for i in range(MLA_HEADS)])
        for i, slab in enumerate(slabs):
            heads_ref[0, i, rows, :] = slab.astype(BF16)
        rest_ref[0, rows, :] = p[:, Q_LORA + KV_LORA + ROPE_TILE:].astype(BF16)


def _inproj(x, mod, g1, win, qg, kvg, wq, wkv, rope_tab, tr):
    nb, n_rows, _ = x.shape
    use_rope = rope_tab is not None
    const = lambda b, t: (0, 0)
    mod_map = (lambda b, t: (b, 0, 0)) if mod.shape[0] == nb else (lambda b, t: (0, 0, 0))
    in_specs = [
        pl.BlockSpec((1, tr, D_MODEL), lambda b, t: (b, t, 0)),
        pl.BlockSpec((1, 6, D_MODEL), mod_map),
        pl.BlockSpec((1, D_MODEL), const),
        pl.BlockSpec((D_MODEL, P_COLS), const),
        pl.BlockSpec((1, Q_LORA), const),
        pl.BlockSpec((1, KV_LORA), const),
        pl.BlockSpec((Q_LORA, Q_COLS), const),
        pl.BlockSpec((KV_LORA, 2 * MLA_W), const),
    ]
    args = [x, mod, g1, win, qg, kvg, wq, wkv]
    if use_rope:
        in_specs.append(pl.BlockSpec((tr, 3 * LANE), lambda b, t: (t, 0)))
        args.append(rope_tab)
    return pl.pallas_call(
        functools.partial(_inproj_kernel, use_rope=use_rope),
        grid=(nb, n_rows // tr),
        in_specs=in_specs,
        out_specs=[pl.BlockSpec((1, N_SLABS, tr, LANE), lambda b, t: (b, 0, t, 0)),
                   pl.BlockSpec((1, tr, REST_COLS), lambda b, t: (b, t, 0))],
        out_shape=[jax.ShapeDtypeStruct((nb, N_SLABS, n_rows, LANE), BF16),
                   jax.ShapeDtypeStruct((nb, n_rows, REST_COLS), BF16)],
        compiler_params=_params(("arbitrary", "arbitrary")),
        name="inproj_rope" if use_rope else "inproj_ctx",
    )(*args)


def _attn_kernel(*refs, n_src, tq):
    qn_ref, qr_ref = refs[0], refs[1]
    src = refs[2:2 + 3 * n_src]
    o_ref = refs[2 + 3 * n_src]
    k_scr, v_scr = refs[3 + 3 * n_src], refs[4 + 3 * n_src]

    off = 0
    for i in range(n_src):
        kn_ref, kr_ref, v_ref = src[3 * i:3 * i + 3]
        n = kn_ref.shape[2]
        lane = lax.broadcasted_iota(jnp.int32, (n, ROPE_TILE), 1)
        keep = (lane // QK_ROPE) == (pl.program_id(1) % 2)
        k_scr[off:off + n, :QK_NOPE] = kn_ref[0, 0]
        k_scr[off:off + n, QK_NOPE:] = jnp.where(keep, kr_ref[0, 0].astype(F32),
                                                 0.0).astype(BF16)
        v_scr[off:off + n, :] = v_ref[0, 0]
        off += n

    for t in range(qn_ref.shape[2] // tq):
        rows = slice(t * tq, (t + 1) * tq)
        q = jnp.concatenate([qn_ref[0, 0, rows, :], qr_ref[0, 0, rows, :]], axis=1)
        s = lax.dot_general(q, k_scr[...], (((1,), (1,)), ((), ())),
                            preferred_element_type=F32)
        e = jnp.exp2(s - jnp.max(s, axis=1, keepdims=True))
        l = jnp.sum(e, axis=1, keepdims=True)
        o = _dot(e.astype(BF16), v_scr[...])
        o_ref[0, 0, rows, :] = (o / l).astype(BF16)


def _attention(q_heads, kv_list, tq):
    nb, _, n_q, _ = q_heads.shape
    slab = lambda n, first, step=1: pl.BlockSpec(
        (1, 1, n, LANE), lambda b, h: (b, first + h // step if step else first, 0, 0))
    in_specs = [slab(n_q, SLAB_QN), slab(n_q, SLAB_QR, ROPE_TILE // QK_ROPE)]
    args = [q_heads, q_heads]
    n_keys = 0
    for kv in kv_list:
        n = kv.shape[2]
        n_keys += n
        in_specs += [slab(n, SLAB_KN), slab(n, SLAB_KR, 0), slab(n, SLAB_V)]
        args += [kv, kv, kv]
    return pl.pallas_call(
        functools.partial(_attn_kernel, n_src=len(kv_list), tq=tq),
        grid=(nb, MLA_HEADS),
        in_specs=in_specs,
        out_specs=pl.BlockSpec((1, 1, n_q, V_DIM), lambda b, h: (b, h, 0, 0)),
        out_shape=jax.ShapeDtypeStruct((nb, MLA_HEADS, n_q, V_DIM), BF16),
        scratch_shapes=[pltpu.VMEM((n_keys, QK_NOPE + ROPE_TILE), BF16),
                        pltpu.VMEM((n_keys, V_DIM), BF16)],
        compiler_params=_params(("arbitrary", "arbitrary")),
        name="attn_%d" % len(kv_list),
    )(*args)


def _merge_kernel(att_ref, rest_ref, x_ref, mod_ref, chan_ref, dft_ref, cw_ref, cb_ref,
                  gn_ref, wout_ref, o_ref, ab_scr, yc_scr, *, seq_len, tr):
    t = pl.program_id(1)

    @pl.when(t == 0)
    def _():
        ab = _dot(rest_ref[0, :, :FNET_W], chan_ref[...])
        ab_scr[0:seq_len, :] = ab[:, :FNET_W].astype(BF16)
        ab_scr[seq_len:2 * seq_len, :] = ab[:, FNET_W:].astype(BF16)
        bg = rest_ref[0, :, FNET_W:FNET_W + CONV_W].astype(F32)
        cg = rest_ref[0, :, FNET_W + CONV_W:FNET_W + 2 * CONV_W].astype(F32)
        xv = rest_ref[0, :, FNET_W + 2 * CONV_W:].astype(F32)
        yc = bg * _conv3_rows(cg * xv, cw_ref, cb_ref, seq_len)
        yc_scr[...] = _rms(yc, gn_ref[:, MLA_W + FNET_W:]).astype(BF16)

    sub = min(tr, CHAIN_ROWS)
    for c in range(tr // sub):
        rows = slice(c * sub, (c + 1) * sub)
        yf = _dot(dft_ref[rows, :], ab_scr[...])
        yf = _rms(yf, gn_ref[:, MLA_W:MLA_W + FNET_W]).astype(BF16)
        att = jnp.concatenate([att_ref[0, i, rows, :] for i in range(MLA_HEADS)], axis=1)
        ya = _rms(att.astype(F32), gn_ref[:, :MLA_W]).astype(BF16)
        r0 = pl.multiple_of(t * tr + c * sub, sub)
        y = jnp.concatenate([ya, yf, yc_scr[pl.ds(r0, sub), :]], axis=1)
        o_ref[0, rows, :] = x_ref[0, rows, :] + mod_ref[0, 2:3, :] * _dot(y, wout_ref[...])


def _merge(att, rest, x, mod, chan_dft, pos_dft, cw, cb, gn, wout, tr):
    nb, seq_len, _ = x.shape
    const = lambda b, t: (0, 0)
    return pl.pallas_call(
        functools.partial(_merge_kernel, seq_len=seq_len, tr=tr),
        grid=(nb, seq_len // tr),
        in_specs=[
            pl.BlockSpec((1, MLA_HEADS, tr, V_DIM), lambda b, t: (b, 0, t, 0)),
            pl.BlockSpec((1, seq_len, REST_COLS), lambda b, t: (b, 0, 0)),
            pl.BlockSpec((1, tr, D_MODEL), lambda b, t: (b, t, 0)),
            pl.BlockSpec((1, 6, D_MODEL), lambda b, t: (b, 0, 0)),
            pl.BlockSpec((FNET_W, 2 * FNET_W), const),
            pl.BlockSpec((tr, 2 * seq_len), lambda b, t: (t, 0)),
            pl.BlockSpec((3, CONV_W), const),
            pl.BlockSpec((1, CONV_W), const),
            pl.BlockSpec((1, D_MIX), const),
            pl.BlockSpec((D_MIX, D_MODEL), const),
        ],
        out_specs=pl.BlockSpec((1, tr, D_MODEL), lambda b, t: (b, t, 0)),
        out_shape=jax.ShapeDtypeStruct(x.shape, F32),
        scratch_shapes=[pltpu.VMEM((2 * seq_len, FNET_W), BF16),
                        pltpu.VMEM((seq_len, CONV_W), BF16)],
        compiler_params=_params(("arbitrary", "arbitrary")),
        name="merge_%d" % seq_len,
    )(att, rest, x, mod, chan_dft, pos_dft, cw, cb, gn, wout)


def _ffn_kernel(*refs, seq_len, final, group_rows):
    if final:
        (x_ref, mod_ref, g2_ref, wg_ref, wv_ref, cw_ref, cb_ref, wd_ref, fg_ref,
         o_ref, h_scr) = refs
    else:
        (x_ref, mod_ref, g2_ref, wg_ref, wv_ref, cw_ref, cb_ref, wd_ref,
         o_ref, h_scr) = refs
    j = pl.program_id(1)
    n_rows = x_ref.shape[1]

    @pl.when(j == 0)
    def _():
        h = _rms(x_ref[0], g2_ref[...]) * (1.0 + mod_ref[0, 4:5, :]) + mod_ref[0, 3:4, :]
        h_scr[...] = h.astype(BF16)
        o_ref[0] = jnp.zeros(o_ref.shape[1:], F32)

    for r0 in range(0, n_rows, group_rows):
        r1 = r0 + group_rows
        e0, e1 = max(r0 - SUBLANE, 0), min(r1 + SUBLANE, n_rows)
        h = h_scr[e0:e1, :]
        gate = _conv3_rows(_dot(h, wg_ref[0]), cw_ref, cb_ref, seq_len, row0=e0)
        act = gate * jax.nn.sigmoid(gate) * _dot(h, wv_ref[0])
        o_ref[0, r0:r1, :] += _dot(act[r0 - e0:r1 - e0].astype(BF16), wd_ref[...])

    @pl.when(j == pl.num_programs(1) - 1)
    def _():
        y = x_ref[0] + mod_ref[0, 5:6, :] * o_ref[0]
        if final:
            y = _rms(y, fg_ref[...])
        o_ref[0] = y


def _ffn(x, mod, g2, wup, cw, cb, wdown, seq_len, final_g, fc):
    nb, n_rows, _ = x.shape
    n_chunk = D_FF // fc
    final = final_g is not None
    const = lambda b, j: (0, 0)
    in_specs = [
        pl.BlockSpec((1, n_rows, D_MODEL), lambda b, j: (b, 0, 0)),
        pl.BlockSpec((1, 6, D_MODEL), lambda b, j: (b, 0, 0)),
        pl.BlockSpec((1, D_MODEL), const),
        pl.BlockSpec((1, D_MODEL, fc), lambda b, j: (j, 0, 0)),
        pl.BlockSpec((1, D_MODEL, fc), lambda b, j: (n_chunk + j, 0, 0)),
        pl.BlockSpec((3, fc), lambda b, j: (0, j)),
        pl.BlockSpec((1, fc), lambda b, j: (0, j)),
        pl.BlockSpec((fc, D_MODEL), lambda b, j: (j, 0)),
    ]
    args = [x, mod, g2, wup, wup, cw, cb, wdown]
    if final:
        in_specs.append(pl.BlockSpec((1, D_MODEL), const))
        args.append(final_g)
    return pl.pallas_call(
        functools.partial(_ffn_kernel, seq_len=seq_len, final=final, group_rows=FFN_GROUP_ROWS),
        grid=(nb, n_chunk),
        in_specs=in_specs,
        out_specs=pl.BlockSpec((1, n_rows, D_MODEL), lambda b, j: (b, 0, 0)),
        out_shape=jax.ShapeDtypeStruct(x.shape, F32),
        scratch_shapes=[pltpu.VMEM((n_rows, D_MODEL), BF16)],
        compiler_params=_params(("arbitrary", "arbitrary")),
        name="ffn_final" if final else "ffn_%d" % seq_len,
    )(*args)


def _rope_perm():
    quarter = QK_ROPE // 4
    idx = [a * 2 * quarter + j * quarter + i
           for j in range(2) for a in range(2) for i in range(quarter)]
    return np.asarray(idx, np.int32)


@functools.lru_cache(maxsize=None)
def _rope_table(n_tokens):
    t = np.arange(n_tokens)
    half = QK_ROPE // 2
    inv = ROPE_THETA ** (-np.arange(0, half, 2, dtype=np.float64) / half)
    ang = np.concatenate([(t // GRID_W)[:, None] * inv, (t % GRID_W)[:, None] * inv], axis=1)
    cos, sin, zero = np.cos(ang), np.sin(ang), np.zeros_like(ang)
    per_head = lambda a, b: np.tile(np.concatenate([a, b], axis=1), (1, ROPE_TILE // QK_ROPE))
    tab = np.concatenate([per_head(cos, cos), per_head(zero, sin), per_head(-sin, zero)], axis=1)
    return tab.astype(np.float32)


@functools.lru_cache(maxsize=None)
def _chan_dft():
    gw = FNET_W // FNET_GROUPS
    ang = 2.0 * np.pi * ((np.arange(gw)[:, None] * np.arange(gw)[None, :]) % gw) / gw
    eye = np.eye(FNET_GROUPS)
    return np.concatenate([np.kron(eye, np.cos(ang)), np.kron(eye, np.sin(ang))],
                          axis=1) * gw ** -0.5


@functools.lru_cache(maxsize=None)
def _pos_dft(seq_len):
    ang = 2.0 * np.pi * ((np.arange(seq_len)[:, None] * np.arange(seq_len)[None, :])
                         % seq_len) / seq_len
    return np.concatenate([np.cos(ang), -np.sin(ang)], axis=1) * seq_len ** -0.5


def kernel(x, c, ctx, c_ctx, ada_w, ada_b, norm1_g, w_in, q_norm_g, kv_norm_g, w_uq, w_ukv,
           sconv_w, sconv_b, out_norm_g, w_out, norm2_g, w_up, ffconv_w, ffconv_b, w_down,
           final_g):
    nb, seq, _ = x.shape
    n_ctx = ctx.shape[1]

    cond = jnp.concatenate([c, c_ctx[None, :]], axis=0)
    cond = jnp.pad(cond, ((0, MOD_ROWS - cond.shape[0]), (0, 0)))
    mod = _ada_mod(cond, ada_w, ada_b)

    perm = _rope_perm()
    rope_tab = jnp.asarray(_rope_table(seq))
    chan_dft = jnp.asarray(_chan_dft(), F32).astype(BF16)
    pos_dft_lat = jnp.asarray(_pos_dft(seq), F32).astype(BF16)
    pos_dft_ctx = jnp.asarray(_pos_dft(n_ctx), F32).astype(BF16)

    x_lat = x
    x_ctx = ctx
    fc = 256
    row = lambda v: v.reshape(1, -1)
    for l in range(DEPTH):
        last = l == DEPTH - 1
        mod_l = mod[l, :nb].reshape(nb, 6, D_MODEL)
        mod_c = mod[l, nb:nb + 1].reshape(1, 6, D_MODEL)

        k_r = w_in[l][:, Q_LORA + KV_LORA:Q_LORA + KV_LORA + QK_ROPE][:, perm]
        win = jnp.concatenate([w_in[l][:, :Q_LORA + KV_LORA], k_r, k_r,
                               w_in[l][:, Q_LORA + KV_LORA + QK_ROPE:]], axis=1).astype(BF16)
        wq = jnp.concatenate([w_uq[l][:, :, :QK_NOPE].reshape(Q_LORA, -1),
                              w_uq[l][:, :, QK_NOPE:][:, :, perm].reshape(Q_LORA, -1)],
                             axis=1).astype(BF16)
        wkv = jnp.concatenate([w_ukv[l][:, :, :QK_NOPE].reshape(KV_LORA, -1),
                               w_ukv[l][:, :, QK_NOPE:].reshape(KV_LORA, -1)],
                              axis=1).astype(BF16)
        wout = w_out[l].astype(BF16)
        wup = w_up[l].astype(BF16).reshape(D_MODEL, 2 * D_FF // fc, fc).transpose(1, 0, 2)
        wdown = w_down[l].astype(BF16)
        proj = (row(norm1_g[l]), win, row(q_norm_g[l]), row(kv_norm_g[l]), wq, wkv)
        mix = (sconv_w[l], row(sconv_b[l]), row(out_norm_g[l]), wout)
        ffn = (row(norm2_g[l]), wup, ffconv_w[l], row(ffconv_b[l]), wdown)

        heads_l, rest_l = _inproj(x_lat, mod_l, *proj, rope_tab, tr=512)
        heads_c, rest_c = _inproj(x_ctx, mod_c, *proj, None, tr=n_ctx)

        att_l = _attention(heads_l, [heads_c, heads_l], tq=256)
        x_lat = _merge(att_l, rest_l, x_lat, mod_l, chan_dft, pos_dft_lat, *mix, tr=512)
        x_lat = _ffn(x_lat, mod_l, *ffn, seq_len=seq,
                     final_g=row(final_g) if last else None, fc=fc)
        if not last:
            att_c = _attention(heads_c, [heads_c], tq=n_ctx)
            x_ctx = _merge(att_c, rest_c, x_ctx, jnp.broadcast_to(mod_c, (nb, 6, D_MODEL)),
                           chan_dft, pos_dft_ctx, *mix, tr=n_ctx)
            x_ctx = _ffn(x_ctx.reshape(1, nb * n_ctx, D_MODEL), mod_c, *ffn,
                         seq_len=n_ctx, final_g=None, fc=fc).reshape(nb, n_ctx, D_MODEL)
    return x_lat
```

```python
import functools

import jax
import jax.numpy as jnp
import numpy as np
from jax import lax
from jax.experimental import pallas as pl
from jax.experimental.pallas import tpu as pltpu

D_MODEL = 1024
DEPTH = 2
GRID_W = 64
MLA_HEADS = 4
QK_NOPE = 128
QK_ROPE = 64
V_DIM = 128
Q_LORA = 384
KV_LORA = 256
MLA_W = MLA_HEADS * V_DIM
FNET_GROUPS = 4
FNET_W = 256
CONV_W = 256
D_MIX = MLA_W + FNET_W + CONV_W
D_FF = 2816
ROPE_THETA = 10000.0
EPS = 1e-6
SM_SCALE = (QK_NOPE + QK_ROPE) ** -0.5
Q_SCALE = SM_SCALE * float(np.log2(np.e))

LANE = 128
SUBLANE = 8
ROPE_TILE = 2 * QK_ROPE
P_COLS = Q_LORA + KV_LORA + ROPE_TILE + FNET_W + 3 * CONV_W
Q_COLS = MLA_HEADS * QK_NOPE + MLA_HEADS * QK_ROPE
SLAB_QN, SLAB_QR = 0, MLA_HEADS
SLAB_KN = SLAB_QR + MLA_HEADS * QK_ROPE // ROPE_TILE
SLAB_KR = SLAB_KN + MLA_HEADS
SLAB_V = SLAB_KR + 1
N_SLABS = SLAB_V + MLA_HEADS
REST_COLS = FNET_W + 3 * CONV_W
CHAIN_ROWS = 256
FFN_GROUP_ROWS = 1024
MOD_ROWS = 16
VMEM_LIMIT = 56 * 1024 * 1024

BF16 = jnp.bfloat16
F32 = jnp.float32


def _params(sem):
    return pltpu.CompilerParams(dimension_semantics=sem, vmem_limit_bytes=VMEM_LIMIT)


def _rms(x, g):
    return x * lax.rsqrt(jnp.mean(x * x, axis=-1, keepdims=True) + EPS) * g


def _dot(a, b):
    return jnp.dot(a, b, preferred_element_type=F32)


def _conv3_rows(z, w_ref, b_ref, seq_len, row0=0):
    n = z.shape[0]
    row = (lax.broadcasted_iota(jnp.int32, z.shape, 0) + row0) % seq_len
    prev = jnp.where(row == 0, 0.0, pltpu.roll(z, 1, axis=0))
    nxt = jnp.where(row == seq_len - 1, 0.0, pltpu.roll(z, n - 1, axis=0))
    return prev * w_ref[0:1, :] + z * w_ref[1:2, :] + nxt * w_ref[2:3, :] + b_ref[...]


def _ada_kernel(c_ref, w_ref, b_ref, o_ref):
    c = c_ref[...]
    s = (c * jax.nn.sigmoid(c)).astype(BF16)
    o_ref[0] = _dot(s, w_ref[0].astype(BF16)) + b_ref[0]


def _ada_mod(cond, ada_w, ada_b):
    n_col = 6 * D_MODEL
    bc = 1536
    return pl.pallas_call(
        _ada_kernel,
        grid=(DEPTH, n_col // bc),
        in_specs=[
            pl.BlockSpec((MOD_ROWS, D_MODEL), lambda l, j: (0, 0)),
            pl.BlockSpec((1, D_MODEL, bc), lambda l, j: (l, 0, j)),
            pl.BlockSpec((1, 1, bc), lambda l, j: (l, 0, j)),
        ],
        out_specs=pl.BlockSpec((1, MOD_ROWS, bc), lambda l, j: (l, 0, j)),
        out_shape=jax.ShapeDtypeStruct((DEPTH, MOD_ROWS, n_col), F32),
        compiler_params=_params(("arbitrary", "arbitrary")),
        name="ada_mod",
    )(cond, ada_w, ada_b.reshape(DEPTH, 1, n_col))


def _rope_tile(t, tab):
    return (t * tab[:, 0:LANE]
            + pltpu.roll(t, QK_ROPE // 2, axis=1) * tab[:, LANE:2 * LANE]
            + pltpu.roll(t, LANE - QK_ROPE // 2, axis=1) * tab[:, 2 * LANE:3 * LANE])


def _inproj_kernel(*refs, use_rope):
    if use_rope:
        (x_ref, mod_ref, g1_ref, win_ref, qg_ref, kvg_ref, wq_ref, wkv_ref, tab_ref,
         heads_ref, rest_ref) = refs
    else:
        (x_ref, mod_ref, g1_ref, win_ref, qg_ref, kvg_ref, wq_ref, wkv_ref,
         heads_ref, rest_ref) = refs
    n_nope = MLA_HEADS * QK_NOPE
    tr = x_ref.shape[1]
    sub = min(tr, CHAIN_ROWS)
    for c in range(tr // sub):
        rows = slice(c * sub, (c + 1) * sub)
        x = x_ref[0, rows, :]
        h = _rms(x, g1_ref[...]) * (1.0 + mod_ref[0, 1:2, :]) + mod_ref[0, 0:1, :]
        p = _dot(h.astype(BF16), win_ref[...])
        cq = _rms(p[:, :Q_LORA], qg_ref[...]).astype(BF16)
        ckv = _rms(p[:, Q_LORA:Q_LORA + KV_LORA], kvg_ref[...]).astype(BF16)
        q = _dot(cq, wq_ref[...])
        kv = _dot(ckv, wkv_ref[...])
        kr = p[:, Q_LORA + KV_LORA:Q_LORA + KV_LORA + ROPE_TILE]
        qr = [q[:, n_nope + i * LANE:n_nope + (i + 1) * LANE] for i in range(2)]
        if use_rope:
            tab = tab_ref[rows, :]
            kr = _rope_tile(kr, tab)
            qr = [_rope_tile(t, tab) for t in qr]
        slabs = ([q[:, i * LANE:(i + 1) * LANE] * Q_SCALE for i in range(MLA_HEADS)]
                 + [t * Q_SCALE for t in qr]
                 + [kv[:, i * LANE:(i + 1) * LANE] for i in range(MLA_HEADS)]
                 + [kr]
                 + [kv[:, n_nope + i * LANE:n_nope + (i + 1) * LANE] for i in range(MLA_HEADS)])
        for i, slab in enumerate(slabs):
            heads_ref[0, i, rows, :] = slab.astype(BF16)
        rest_ref[0, rows, :] = p[:, Q_LORA + KV_LORA + ROPE_TILE:].astype(BF16)


def _inproj(x, mod, g1, win, qg, kvg, wq, wkv, rope_tab, tr):
    nb, n_rows, _ = x.shape
    use_rope = rope_tab is not None
    const = lambda b, t: (0, 0)
    mod_map = (lambda b, t: (b, 0, 0)) if mod.shape[0] == nb else (lambda b, t: (0, 0, 0))
    in_specs = [
        pl.BlockSpec((1, tr, D_MODEL), lambda b, t: (b, t, 0)),
        pl.BlockSpec((1, 6, D_MODEL), mod_map),
        pl.BlockSpec((1, D_MODEL), const),
        pl.BlockSpec((D_MODEL, P_COLS), const),
        pl.BlockSpec((1, Q_LORA), const),
        pl.BlockSpec((1, KV_LORA), const),
        pl.BlockSpec((Q_LORA, Q_COLS), const),
        pl.BlockSpec((KV_LORA, 2 * MLA_W), const),
    ]
    args = [x, mod, g1, win, qg, kvg, wq, wkv]
    if use_rope:
        in_specs.append(pl.BlockSpec((tr, 3 * LANE), lambda b, t: (t, 0)))
        args.append(rope_tab)
    return pl.pallas_call(
        functools.partial(_inproj_kernel, use_rope=use_rope),
        grid=(nb, n_rows // tr),
        in_specs=in_specs,
        out_specs=[pl.BlockSpec((1, N_SLABS, tr, LANE), lambda b, t: (b, 0, t, 0)),
                   pl.BlockSpec((1, tr, REST_COLS), lambda b, t: (b, t, 0))],
        out_shape=[jax.ShapeDtypeStruct((nb, N_SLABS, n_rows, LANE), BF16),
                   jax.ShapeDtypeStruct((nb, n_rows, REST_COLS), BF16)],
        compiler_params=_params(("arbitrary", "arbitrary")),
        name="inproj_rope" if use_rope else "inproj_ctx",
    )(*args)


def _attn_kernel(*refs, n_src, tq):
    qn_ref, qr_ref = refs[0], refs[1]
    src = refs[2:2 + 3 * n_src]
    o_ref = refs[2 + 3 * n_src]
    k_scr, v_scr = refs[3 + 3 * n_src], refs[4 + 3 * n_src]

    off = 0
    for i in range(n_src):
        kn_ref, kr_ref, v_ref = src[3 * i:3 * i + 3]
        n = kn_ref.shape[2]
        lane = lax.broadcasted_iota(jnp.int32, (n, ROPE_TILE), 1)
        keep = (lane // QK_ROPE) == (pl.program_id(1) % 2)
        k_scr[off:off + n, :QK_NOPE] = kn_ref[0, 0]
        k_scr[off:off + n, QK_NOPE:] = jnp.where(keep, kr_ref[0, 0].astype(F32),
                                                 0.0).astype(BF16)
        v_scr[off:off + n, :] = v_ref[0, 0]
        off += n

    for t in range(qn_ref.shape[2] // tq):
        rows = slice(t * tq, (t + 1) * tq)
        q = jnp.concatenate([qn_ref[0, 0, rows, :], qr_ref[0, 0, rows, :]], axis=1)
        s = lax.dot_general(q, k_scr[...], (((1,), (1,)), ((), ())),
                            preferred_element_type=F32)
        e = jnp.exp2(s - jnp.max(s, axis=1, keepdims=True))
        l = jnp.sum(e, axis=1, keepdims=True)
        o = _dot(e.astype(BF16), v_scr[...])
        o_ref[0, 0, rows, :] = (o / l).astype(BF16)


def _attention(q_heads, kv_list, tq):
    nb, _, n_q, _ = q_heads.shape
    slab = lambda n, first, step=1: pl.BlockSpec(
        (1, 1, n, LANE), lambda b, h: (b, first + h // step if step else first, 0, 0))
    in_specs = [slab(n_q, SLAB_QN), slab(n_q, SLAB_QR, ROPE_TILE // QK_ROPE)]
    args = [q_heads, q_heads]
    n_keys = 0
    for kv in kv_list:
        n = kv.shape[2]
        n_keys += n
        in_specs += [slab(n, SLAB_KN), slab(n, SLAB_KR, 0), slab(n, SLAB_V)]
        args += [kv, kv, kv]
    return pl.pallas_call(
        functools.partial(_attn_kernel, n_src=len(kv_list), tq=tq),
        grid=(nb, MLA_HEADS),
        in_specs=in_specs,
        out_specs=pl.BlockSpec((1, 1, n_q, V_DIM), lambda b, h: (b, h, 0, 0)),
        out_shape=jax.ShapeDtypeStruct((nb, MLA_HEADS, n_q, V_DIM), BF16),
        scratch_shapes=[pltpu.VMEM((n_keys, QK_NOPE + ROPE_TILE), BF16),
                        pltpu.VMEM((n_keys, V_DIM), BF16)],
        compiler_params=_params(("arbitrary", "arbitrary")),
        name="attn_%d" % len(kv_list),
    )(*args)


def _merge_kernel(att_ref, rest_ref, x_ref, mod_ref, chan_ref, dft_ref, cw_ref, cb_ref,
                  gn_ref, wout_ref, o_ref, ab_scr, yc_scr, *, seq_len, tr):
    t = pl.program_id(1)

    @pl.when(t == 0)
    def _():
        ab = _dot(rest_ref[0, :, :FNET_W], chan_ref[...])
        ab_scr[0:seq_len, :] = ab[:, :FNET_W].astype(BF16)
        ab_scr[seq_len:2 * seq_len, :] = ab[:, FNET_W:].astype(BF16)
        bg = rest_ref[0, :, FNET_W:FNET_W + CONV_W].astype(F32)
        cg = rest_ref[0, :, FNET_W + CONV_W:FNET_W + 2 * CONV_W].astype(F32)
        xv = rest_ref[0, :, FNET_W + 2 * CONV_W:].astype(F32)
        yc = bg * _conv3_rows(cg * xv, cw_ref, cb_ref, seq_len)
        yc_scr[...] = _rms(yc, gn_ref[:, MLA_W + FNET_W:]).astype(BF16)

    sub = min(tr, CHAIN_ROWS)
    for c in range(tr // sub):
        rows = slice(c * sub, (c + 1) * sub)
        yf = _dot(dft_ref[rows, :], ab_scr[...])
        yf = _rms(yf, gn_ref[:, MLA_W:MLA_W + FNET_W]).astype(BF16)
        att = jnp.concatenate([att_ref[0, i, rows, :] for i in range(MLA_HEADS)], axis=1)
        ya = _rms(att.astype(F32), gn_ref[:, :MLA_W]).astype(BF16)
        r0 = pl.multiple_of(t * tr + c * sub, sub)
        y = jnp.concatenate([ya, yf, yc_scr[pl.ds(r0, sub), :]], axis=1)
        o_ref[0, rows, :] = x_ref[0, rows, :] + mod_ref[0, 2:3, :] * _dot(y, wout_ref[...])


def _merge(att, rest, x, mod, chan_dft, pos_dft, cw, cb, gn, wout, tr):
    nb, seq_len, _ = x.shape
    const = lambda b, t: (0, 0)
    return pl.pallas_call(
        functools.partial(_merge_kernel, seq_len=seq_len, tr=tr),
        grid=(nb, seq_len // tr),
        in_specs=[
            pl.BlockSpec((1, MLA_HEADS, tr, V_DIM), lambda b, t: (b, 0, t, 0)),
            pl.BlockSpec((1, seq_len, REST_COLS), lambda b, t: (b, 0, 0)),
            pl.BlockSpec((1, tr, D_MODEL), lambda b, t: (b, t, 0)),
            pl.BlockSpec((1, 6, D_MODEL), lambda b, t: (b, 0, 0)),
            pl.BlockSpec((FNET_W, 2 * FNET_W), const),
            pl.BlockSpec((tr, 2 * seq_len), lambda b, t: (t, 0)),
            pl.BlockSpec((3, CONV_W), const),
            pl.BlockSpec((1, CONV_W), const),
            pl.BlockSpec((1, D_MIX), const),
            pl.BlockSpec((D_MIX, D_MODEL), const),
        ],
        out_specs=pl.BlockSpec((1, tr, D_MODEL), lambda b, t: (b, t, 0)),
        out_shape=jax.ShapeDtypeStruct(x.shape, F32),
        scratch_shapes=[pltpu.VMEM((2 * seq_len, FNET_W), BF16),
                        pltpu.VMEM((seq_len, CONV_W), BF16)],
        compiler_params=_params(("arbitrary", "arbitrary")),
        name="merge_%d" % seq_len,
    )(att, rest, x, mod, chan_dft, pos_dft, cw, cb, gn, wout)


def _ffn_kernel(*refs, seq_len, final, group_rows):
    if final:
        (x_ref, mod_ref, g2_ref, wg_ref, wv_ref, cw_ref, cb_ref, wd_ref, fg_ref,
         o_ref, h_scr) = refs
    else:
        (x_ref, mod_ref, g2_ref, wg_ref, wv_ref, cw_ref, cb_ref, wd_ref,
         o_ref, h_scr) = refs
    j = pl.program_id(1)
    n_rows = x_ref.shape[1]

    @pl.when(j == 0)
    def _():
        h = _rms(x_ref[0], g2_ref[...]) * (1.0 + mod_ref[0, 4:5, :]) + mod_ref[0, 3:4, :]
        h_scr[...] = h.astype(BF16)
        o_ref[0] = jnp.zeros(o_ref.shape[1:], F32)

    for r0 in range(0, n_rows, group_rows):
        r1 = r0 + group_rows
        e0, e1 = max(r0 - SUBLANE, 0), min(r1 + SUBLANE, n_rows)
        h = h_scr[e0:e1, :]
        gate = _conv3_rows(_dot(h, wg_ref[0]), cw_ref, cb_ref, seq_len, row0=e0)
        act = gate * jax.nn.sigmoid(gate) * _dot(h, wv_ref[0])
        o_ref[0, r0:r1, :] += _dot(act[r0 - e0:r1 - e0].astype(BF16), wd_ref[...])

    @pl.when(j == pl.num_programs(1) - 1)
    def _():
        y = x_ref[0] + mod_ref[0, 5:6, :] * o_ref[0]
        if final:
            y = _rms(y, fg_ref[...])
        o_ref[0] = y


def _ffn(x, mod, g2, wup, cw, cb, wdown, seq_len, final_g, fc):
    nb, n_rows, _ = x.shape
    n_chunk = D_FF // fc
    final = final_g is not None
    const = lambda b, j: (0, 0)
    in_specs = [
        pl.BlockSpec((1, n_rows, D_MODEL), lambda b, j: (b, 0, 0)),
        pl.BlockSpec((1, 6, D_MODEL), lambda b, j: (b, 0, 0)),
        pl.BlockSpec((1, D_MODEL), const),
        pl.BlockSpec((1, D_MODEL, fc), lambda b, j: (j, 0, 0)),
        pl.BlockSpec((1, D_MODEL, fc), lambda b, j: (n_chunk + j, 0, 0)),
        pl.BlockSpec((3, fc), lambda b, j: (0, j)),
        pl.BlockSpec((1, fc), lambda b, j: (0, j)),
        pl.BlockSpec((fc, D_MODEL), lambda b, j: (j, 0)),
    ]
    args = [x, mod, g2, wup, wup, cw, cb, wdown]
    if final:
        in_specs.append(pl.BlockSpec((1, D_MODEL), const))
        args.append(final_g)
    return pl.pallas_call(
        functools.partial(_ffn_kernel, seq_len=seq_len, final=final, group_rows=FFN_GROUP_ROWS),
        grid=(nb, n_chunk),
        in_specs=in_specs,
        out_specs=pl.BlockSpec((1, n_rows, D_MODEL), lambda b, j: (b, 0, 0)),
        out_shape=jax.ShapeDtypeStruct(x.shape, F32),
        scratch_shapes=[pltpu.VMEM((n_rows, D_MODEL), BF16)],
        compiler_params=_params(("arbitrary", "arbitrary")),
        name="ffn_final" if final else "ffn_%d" % seq_len,
    )(*args)


def _rope_perm():
    quarter = QK_ROPE // 4
    idx = [a * 2 * quarter + j * quarter + i
           for j in range(2) for a in range(2) for i in range(quarter)]
    return np.asarray(idx, np.int32)


@functools.lru_cache(maxsize=None)
def _rope_table(n_tokens):
    t = np.arange(n_tokens)
    half = QK_ROPE // 2
    inv = ROPE_THETA ** (-np.arange(0, half, 2, dtype=np.float64) / half)
    ang = np.concatenate([(t // GRID_W)[:, None] * inv, (t % GRID_W)[:, None] * inv], axis=1)
    cos, sin, zero = np.cos(ang), np.sin(ang), np.zeros_like(ang)
    per_head = lambda a, b: np.tile(np.concatenate([a, b], axis=1), (1, ROPE_TILE // QK_ROPE))
    tab = np.concatenate([per_head(cos, cos), per_head(zero, sin), per_head(-sin, zero)], axis=1)
    return tab.astype(np.float32)


@functools.lru_cache(maxsize=None)
def _chan_dft():
    gw = FNET_W // FNET_GROUPS
    ang = 2.0 * np.pi * ((np.arange(gw)[:, None] * np.arange(gw)[None, :]) % gw) / gw
    eye = np.eye(FNET_GROUPS)
    return np.concatenate([np.kron(eye, np.cos(ang)), np.kron(eye, np.sin(ang))],
                          axis=1) * gw ** -0.5


@functools.lru_cache(maxsize=None)
def _pos_dft(seq_len):
    ang = 2.0 * np.pi * ((np.arange(seq_len)[:, None] * np.arange(seq_len)[None, :])
                         % seq_len) / seq_len
    return np.concatenate([np.cos(ang), -np.sin(ang)], axis=1) * seq_len ** -0.5


def kernel(x, c, ctx, c_ctx, ada_w, ada_b, norm1_g, w_in, q_norm_g, kv_norm_g, w_uq, w_ukv,
           sconv_w, sconv_b, out_norm_g, w_out, norm2_g, w_up, ffconv_w, ffconv_b, w_down,
           final_g):
    nb, seq, _ = x.shape
    n_ctx = ctx.shape[1]

    cond = jnp.concatenate([c, c_ctx[None, :]], axis=0)
    cond = jnp.pad(cond, ((0, MOD_ROWS - cond.shape[0]), (0, 0)))
    mod = _ada_mod(cond, ada_w, ada_b)

    perm = _rope_perm()
    rope_tab = jnp.asarray(_rope_table(seq))
    chan_dft = jnp.asarray(_chan_dft(), F32).astype(BF16)
    pos_dft_lat = jnp.asarray(_pos_dft(seq), F32).astype(BF16)
    pos_dft_ctx = jnp.asarray(_pos_dft(n_ctx), F32).astype(BF16)

    x_lat = x
    x_ctx = ctx
    fc = 256
    row = lambda v: v.reshape(1, -1)
    for l in range(DEPTH):
        last = l == DEPTH - 1
        mod_l = mod[l, :nb].reshape(nb, 6, D_MODEL)
        mod_c = mod[l, nb:nb + 1].reshape(1, 6, D_MODEL)

        k_r = w_in[l][:, Q_LORA + KV_LORA:Q_LORA + KV_LORA + QK_ROPE][:, perm]
        win = jnp.concatenate([w_in[l][:, :Q_LORA + KV_LORA], k_r, k_r,
                               w_in[l][:, Q_LORA + KV_LORA + QK_ROPE:]], axis=1).astype(BF16)
        wq = jnp.concatenate([w_uq[l][:, :, :QK_NOPE].reshape(Q_LORA, -1),
                              w_uq[l][:, :, QK_NOPE:][:, :, perm].reshape(Q_LORA, -1)],
                             axis=1).astype(BF16)
        wkv = jnp.concatenate([w_ukv[l][:, :, :QK_NOPE].reshape(KV_LORA, -1),
                               w_ukv[l][:, :, QK_NOPE:].reshape(KV_LORA, -1)],
                              axis=1).astype(BF16)
        wout = w_out[l].astype(BF16)
        wup = w_up[l].astype(BF16).reshape(D_MODEL, 2 * D_FF // fc, fc).transpose(1, 0, 2)
        wdown = w_down[l].astype(BF16)
        proj = (row(norm1_g[l]), win, row(q_norm_g[l]), row(kv_norm_g[l]), wq, wkv)
        mix = (sconv_w[l], row(sconv_b[l]), row(out_norm_g[l]), wout)
        ffn = (row(norm2_g[l]), wup, ffconv_w[l], row(ffconv_b[l]), wdown)

        heads_l, rest_l = _inproj(x_lat, mod_l, *proj, rope_tab, tr=512)
        heads_c, rest_c = _inproj(x_ctx, mod_c, *proj, None, tr=n_ctx)

        att_l = _attention(heads_l, [heads_c, heads_l], tq=512)
        x_lat = _merge(att_l, rest_l, x_lat, mod_l, chan_dft, pos_dft_lat, *mix, tr=512)
        x_lat = _ffn(x_lat, mod_l, *ffn, seq_len=seq,
                     final_g=row(final_g) if last else None, fc=fc)
        if not last:
            att_c = _attention(heads_c, [heads_c], tq=n_ctx)
            x_ctx = _merge(att_c, rest_c, x_ctx, jnp.broadcast_to(mod_c, (nb, 6, D_MODEL)),
                           chan_dft, pos_dft_ctx, *mix, tr=n_ctx)
            x_ctx = _ffn(x_ctx.reshape(1, nb * n_ctx, D_MODEL), mod_c, *ffn,
                         seq_len=n_ctx, final_g=None, fc=fc).reshape(nb, n_ctx, D_MODEL)
    return x_lat
```

```python
import functools

import jax
import jax.numpy as jnp
import numpy as np
from jax import lax
from jax.experimental import pallas as pl
from jax.experimental.pallas import tpu as pltpu

D_MODEL = 1024
DEPTH = 2
GRID_W = 64
MLA_HEADS = 4
QK_NOPE = 128
QK_ROPE = 64
V_DIM = 128
Q_LORA = 384
KV_LORA = 256
MLA_W = MLA_HEADS * V_DIM
FNET_GROUPS = 4
FNET_W = 256
CONV_W = 256
D_MIX = MLA_W + FNET_W + CONV_W
D_FF = 2816
ROPE_THETA = 10000.0
EPS = 1e-6
SM_SCALE = (QK_NOPE + QK_ROPE) ** -0.5
Q_SCALE = SM_SCALE * float(np.log2(np.e))

LANE = 128
SUBLANE = 8
ROPE_TILE = 2 * QK_ROPE
P_COLS = Q_LORA + KV_LORA + ROPE_TILE + FNET_W + 3 * CONV_W
Q_COLS = MLA_HEADS * QK_NOPE + MLA_HEADS * QK_ROPE
SLAB_QN, SLAB_QR = 0, MLA_HEADS
SLAB_KN = SLAB_QR + MLA_HEADS * QK_ROPE // ROPE_TILE
SLAB_KR = SLAB_KN + MLA_HEADS
SLAB_V = SLAB_KR + 1
N_SLABS = SLAB_V + MLA_HEADS
REST_COLS = FNET_W + 3 * CONV_W
CHAIN_ROWS = 256
MOD_ROWS = 16
VMEM_LIMIT = 56 * 1024 * 1024

BF16 = jnp.bfloat16
F32 = jnp.float32


def _params(sem):
    return pltpu.CompilerParams(dimension_semantics=sem, vmem_limit_bytes=VMEM_LIMIT)


def _rms(x, g):
    return x * lax.rsqrt(jnp.mean(x * x, axis=-1, keepdims=True) + EPS) * g


def _dot(a, b):
    return jnp.dot(a, b, preferred_element_type=F32)


def _roll_rows(z):
    return pltpu.roll(z, 1, axis=0), pltpu.roll(z, z.shape[0] - 1, axis=0)


def _conv3_rows(prev, z, nxt, w_ref, b_ref, seq_len):
    n = z.shape[0]
    w0, w1, w2 = w_ref[0:1, :], w_ref[1:2, :], w_ref[2:3, :]
    out = prev * w0 + z * w1 + nxt * w2 + b_ref[...]
    sub = lax.broadcasted_iota(jnp.int32, (SUBLANE, z.shape[1]), 0)
    pieces, done = [], 0
    for s in range(0, n, seq_len):
        for r, edge, keep_prev, keep_nxt in ((s, 0, False, True),
                                             (s + seq_len - SUBLANE, SUBLANE - 1, True, False)):
            t = slice(r, r + SUBLANE)
            p = prev[t] if keep_prev else jnp.where(sub == edge, 0.0, prev[t])
            q = nxt[t] if keep_nxt else jnp.where(sub == edge, 0.0, nxt[t])
            pieces += [out[done:r], p * w0 + z[t] * w1 + q * w2 + b_ref[...]]
            done = r + SUBLANE
    return jnp.concatenate([x for x in pieces if x.shape[0]], axis=0)


def _ada_kernel(c_ref, w_ref, b_ref, o_ref):
    c = c_ref[...]
    s = (c * jax.nn.sigmoid(c)).astype(BF16)
    o_ref[0] = _dot(s, w_ref[0].astype(BF16)) + b_ref[0]


def _ada_mod(cond, ada_w, ada_b):
    n_col = 6 * D_MODEL
    bc = 1536
    return pl.pallas_call(
        _ada_kernel,
        grid=(DEPTH, n_col // bc),
        in_specs=[
            pl.BlockSpec((MOD_ROWS, D_MODEL), lambda l, j: (0, 0)),
            pl.BlockSpec((1, D_MODEL, bc), lambda l, j: (l, 0, j)),
            pl.BlockSpec((1, 1, bc), lambda l, j: (l, 0, j)),
        ],
        out_specs=pl.BlockSpec((1, MOD_ROWS, bc), lambda l, j: (l, 0, j)),
        out_shape=jax.ShapeDtypeStruct((DEPTH, MOD_ROWS, n_col), F32),
        compiler_params=_params(("arbitrary", "arbitrary")),
        name="ada_mod",
    )(cond, ada_w, ada_b.reshape(DEPTH, 1, n_col))


def _rope_tile(t, tab):
    return (t * tab[:, 0:LANE]
            + pltpu.roll(t, QK_ROPE // 2, axis=1) * tab[:, LANE:2 * LANE]
            + pltpu.roll(t, LANE - QK_ROPE // 2, axis=1) * tab[:, 2 * LANE:3 * LANE])


def _inproj_kernel(*refs, use_rope):
    if use_rope:
        (x_ref, mod_ref, g1_ref, win_ref, qg_ref, kvg_ref, wq_ref, wkv_ref, tab_ref,
         heads_ref, rest_ref) = refs
    else:
        (x_ref, mod_ref, g1_ref, win_ref, qg_ref, kvg_ref, wq_ref, wkv_ref,
         heads_ref, rest_ref) = refs
    n_nope = MLA_HEADS * QK_NOPE
    tr = x_ref.shape[1]
    sub = min(tr, CHAIN_ROWS)
    for c in range(tr // sub):
        rows = slice(c * sub, (c + 1) * sub)
        x = x_ref[0, rows, :]
        h = _rms(x, g1_ref[...]) * (1.0 + mod_ref[0, 1:2, :]) + mod_ref[0, 0:1, :]
        p = _dot(h.astype(BF16), win_ref[...])
        cq = _rms(p[:, :Q_LORA], qg_ref[...]).astype(BF16)
        ckv = _rms(p[:, Q_LORA:Q_LORA + KV_LORA], kvg_ref[...]).astype(BF16)
        q = _dot(cq, wq_ref[...])
        kv = _dot(ckv, wkv_ref[...])
        kr = p[:, Q_LORA + KV_LORA:Q_LORA + KV_LORA + ROPE_TILE]
        qr = [q[:, n_nope + i * LANE:n_nope + (i + 1) * LANE] for i in range(2)]
        if use_rope:
            tab = tab_ref[rows, :]
            kr = _rope_tile(kr, tab)
            qr = [_rope_tile(t, tab) for t in qr]
        slabs = ([q[:, i * LANE:(i + 1) * LANE] * Q_SCALE for i in range(MLA_HEADS)]
                 + [t * Q_SCALE for t in qr]
                 + [kv[:, i * LANE:(i + 1) * LANE] for i in range(MLA_HEADS)]
                 + [kr]
                 + [kv[:, n_nope + i * LANE:n_nope + (i + 1) * LANE] for i in range(MLA_HEADS)])
        for i, slab in enumerate(slabs):
            heads_ref[0, i, rows, :] = slab.astype(BF16)
        rest_ref[0, rows, :] = p[:, Q_LORA + KV_LORA + ROPE_TILE:].astype(BF16)


def _inproj(x, mod, g1, win, qg, kvg, wq, wkv, rope_tab, tr):
    nb, n_rows, _ = x.shape
    use_rope = rope_tab is not None
    const = lambda b, t: (0, 0)
    mod_map = (lambda b, t: (b, 0, 0)) if mod.shape[0] == nb else (lambda b, t: (0, 0, 0))
    in_specs = [
        pl.BlockSpec((1, tr, D_MODEL), lambda b, t: (b, t, 0)),
        pl.BlockSpec((1, 6, D_MODEL), mod_map),
        pl.BlockSpec((1, D_MODEL), const),
        pl.BlockSpec((D_MODEL, P_COLS), const),
        pl.BlockSpec((1, Q_LORA), const),
        pl.BlockSpec((1, KV_LORA), const),
        pl.BlockSpec((Q_LORA, Q_COLS), const),
        pl.BlockSpec((KV_LORA, 2 * MLA_W), const),
    ]
    args = [x, mod, g1, win, qg, kvg, wq, wkv]
    if use_rope:
        in_specs.append(pl.BlockSpec((tr, 3 * LANE), lambda b, t: (t, 0)))
        args.append(rope_tab)
    return pl.pallas_call(
        functools.partial(_inproj_kernel, use_rope=use_rope),
        grid=(nb, n_rows // tr),
        in_specs=in_specs,
        out_specs=[pl.BlockSpec((1, N_SLABS, tr, LANE), lambda b, t: (b, 0, t, 0)),
                   pl.BlockSpec((1, tr, REST_COLS), lambda b, t: (b, t, 0))],
        out_shape=[jax.ShapeDtypeStruct((nb, N_SLABS, n_rows, LANE), BF16),
                   jax.ShapeDtypeStruct((nb, n_rows, REST_COLS), BF16)],
        compiler_params=_params(("arbitrary", "arbitrary")),
        name="inproj_rope" if use_rope else "inproj_ctx",
    )(*args)


def _attn_kernel(*refs, n_src, tq):
    qn_ref, qr_ref = refs[0], refs[1]
    src = refs[2:2 + 3 * n_src]
    o_ref = refs[2 + 3 * n_src]
    k_scr, v_scr = refs[3 + 3 * n_src], refs[4 + 3 * n_src]

    off = 0
    for i in range(n_src):
        kn_ref, kr_ref, v_ref = src[3 * i:3 * i + 3]
        n = kn_ref.shape[2]
        lane = lax.broadcasted_iota(jnp.int32, (n, ROPE_TILE), 1)
        keep = (lane // QK_ROPE) == (pl.program_id(1) % 2)
        k_scr[off:off + n, :QK_NOPE] = kn_ref[0, 0]
        k_scr[off:off + n, QK_NOPE:] = jnp.where(keep, kr_ref[0, 0].astype(F32),
                                                 0.0).astype(BF16)
        v_scr[off:off + n, :] = v_ref[0, 0]
        off += n

    for t in range(qn_ref.shape[2] // tq):
        rows = slice(t * tq, (t + 1) * tq)
        q = jnp.concatenate([qn_ref[0, 0, rows, :], qr_ref[0, 0, rows, :]], axis=1)
        s = lax.dot_general(q, k_scr[...], (((1,), (1,)), ((), ())),
                            preferred_element_type=F32)
        e = jnp.exp2(s - jnp.max(s, axis=1, keepdims=True))
        l = jnp.sum(e, axis=1, keepdims=True)
        o = _dot(e.astype(BF16), v_scr[...])
        o_ref[0, 0, rows, :] = (o / l).astype(BF16)


def _attention(q_heads, kv_list, tq):
    nb, _, n_q, _ = q_heads.shape
    slab = lambda n, first, step=1: pl.BlockSpec(
        (1, 1, n, LANE), lambda b, h: (b, first + h // step if step else first, 0, 0))
    in_specs = [slab(n_q, SLAB_QN), slab(n_q, SLAB_QR, ROPE_TILE // QK_ROPE)]
    args = [q_heads, q_heads]
    n_keys = 0
    for kv in kv_list:
        n = kv.shape[2]
        n_keys += n
        in_specs += [slab(n, SLAB_KN), slab(n, SLAB_KR, 0), slab(n, SLAB_V)]
        args += [kv, kv, kv]
    return pl.pallas_call(
        functools.partial(_attn_kernel, n_src=len(kv_list), tq=tq),
        grid=(nb, MLA_HEADS),
        in_specs=in_specs,
        out_specs=pl.BlockSpec((1, 1, n_q, V_DIM), lambda b, h: (b, h, 0, 0)),
        out_shape=jax.ShapeDtypeStruct((nb, MLA_HEADS, n_q, V_DIM), BF16),
        scratch_shapes=[pltpu.VMEM((n_keys, QK_NOPE + ROPE_TILE), BF16),
                        pltpu.VMEM((n_keys, V_DIM), BF16)],
        compiler_params=_params(("arbitrary", "arbitrary")),
        name="attn_%d" % len(kv_list),
    )(*args)


def _merge_kernel(att_ref, rest_ref, x_ref, mod_ref, chan_ref, dft_ref, cw_ref, cb_ref,
                  gn_ref, wout_ref, o_ref, ab_scr, yc_scr, *, seq_len, tr):
    t = pl.program_id(1)

    @pl.when(t == 0)
    def _():
        ab = _dot(rest_ref[0, :, :FNET_W], chan_ref[...])
        ab_scr[0:seq_len, :] = ab[:, :FNET_W].astype(BF16)
        ab_scr[seq_len:2 * seq_len, :] = ab[:, FNET_W:].astype(BF16)
        bg = rest_ref[0, :, FNET_W:FNET_W + CONV_W].astype(F32)
        cg = rest_ref[0, :, FNET_W + CONV_W:FNET_W + 2 * CONV_W].astype(F32)
        xv = rest_ref[0, :, FNET_W + 2 * CONV_W:].astype(F32)
        z = cg * xv
        prev, nxt = _roll_rows(z)
        yc = bg * _conv3_rows(prev, z, nxt, cw_ref, cb_ref, seq_len)
        yc_scr[...] = _rms(yc, gn_ref[:, MLA_W + FNET_W:]).astype(BF16)

    sub = min(tr, CHAIN_ROWS)
    for c in range(tr // sub):
        rows = slice(c * sub, (c + 1) * sub)
        yf = _dot(dft_ref[rows, :], ab_scr[...])
        yf = _rms(yf, gn_ref[:, MLA_W:MLA_W + FNET_W]).astype(BF16)
        att = jnp.concatenate([att_ref[0, i, rows, :] for i in range(MLA_HEADS)], axis=1)
        ya = _rms(att.astype(F32), gn_ref[:, :MLA_W]).astype(BF16)
        r0 = pl.multiple_of(t * tr + c * sub, sub)
        y = jnp.concatenate([ya, yf, yc_scr[pl.ds(r0, sub), :]], axis=1)
        o_ref[0, rows, :] = x_ref[0, rows, :] + mod_ref[0, 2:3, :] * _dot(y, wout_ref[...])


def _merge(att, rest, x, mod, chan_dft, pos_dft, cw, cb, gn, wout, tr):
    nb, seq_len, _ = x.shape
    const = lambda b, t: (0, 0)
    return pl.pallas_call(
        functools.partial(_merge_kernel, seq_len=seq_len, tr=tr),
        grid=(nb, seq_len // tr),
        in_specs=[
            pl.BlockSpec((1, MLA_HEADS, tr, V_DIM), lambda b, t: (b, 0, t, 0)),
            pl.BlockSpec((1, seq_len, REST_COLS), lambda b, t: (b, 0, 0)),
            pl.BlockSpec((1, tr, D_MODEL), lambda b, t: (b, t, 0)),
            pl.BlockSpec((1, 6, D_MODEL), lambda b, t: (b, 0, 0)),
            pl.BlockSpec((FNET_W, 2 * FNET_W), const),
            pl.BlockSpec((tr, 2 * seq_len), lambda b, t: (t, 0)),
            pl.BlockSpec((3, CONV_W), const),
            pl.BlockSpec((1, CONV_W), const),
            pl.BlockSpec((1, D_MIX), const),
            pl.BlockSpec((D_MIX, D_MODEL), const),
        ],
        out_specs=pl.BlockSpec((1, tr, D_MODEL), lambda b, t: (b, t, 0)),
        out_shape=jax.ShapeDtypeStruct(x.shape, F32),
        scratch_shapes=[pltpu.VMEM((2 * seq_len, FNET_W), BF16),
                        pltpu.VMEM((seq_len, CONV_W), BF16)],
        compiler_params=_params(("arbitrary", "arbitrary")),
        name="merge_%d" % seq_len,
    )(att, rest, x, mod, chan_dft, pos_dft, cw, cb, gn, wout)


def _ffn_kernel(*refs, seq_len, final):
    if final:
        (x_ref, mod_ref, g2_ref, wg_ref, wv_ref, cw_ref, cb_ref, wd_ref, fg_ref,
         o_ref, h_scr, u_scr) = refs
    else:
        (x_ref, mod_ref, g2_ref, wg_ref, wv_ref, cw_ref, cb_ref, wd_ref,
         o_ref, h_scr, u_scr) = refs
    j = pl.program_id(1)
    n_chunk = pl.num_programs(1) - 1
    fc = wg_ref.shape[2]
    slot = j % 2

    def up_project(dst):
        h = h_scr[...]
        u_scr[dst, :, :fc] = _dot(h, wg_ref[0])
        u_scr[dst, :, fc:] = _dot(h, wv_ref[0])

    def accumulate_down(src):
        z = u_scr[src, :, :fc]
        prev, nxt = _roll_rows(z)
        gate = _conv3_rows(prev, z, nxt, cw_ref, cb_ref, seq_len)
        act = gate * jax.nn.sigmoid(gate) * u_scr[src, :, fc:]
        o_ref[0] += _dot(act.astype(BF16), wd_ref[...])

    @pl.when(j == 0)
    def _():
        h = _rms(x_ref[0], g2_ref[...]) * (1.0 + mod_ref[0, 4:5, :]) + mod_ref[0, 3:4, :]
        h_scr[...] = h.astype(BF16)
        o_ref[0] = jnp.zeros(o_ref.shape[1:], F32)
        up_project(0)

    @pl.when((j > 0) & (j < n_chunk))
    def _():
        accumulate_down(1 - slot)
        up_project(slot)

    @pl.when(j == n_chunk)
    def _():
        accumulate_down(1 - slot)
        y = x_ref[0] + mod_ref[0, 5:6, :] * o_ref[0]
        if final:
            y = _rms(y, fg_ref[...])
        o_ref[0] = y


def _ffn(x, mod, g2, wup, cw, cb, wdown, seq_len, final_g, fc):
    nb, n_rows, _ = x.shape
    n_chunk = D_FF // fc
    final = final_g is not None
    const = lambda b, j: (0, 0)
    up = lambda j: jnp.minimum(j, n_chunk - 1)
    down = lambda j: jnp.maximum(j - 1, 0)
    in_specs = [
        pl.BlockSpec((1, n_rows, D_MODEL), lambda b, j: (b, 0, 0)),
        pl.BlockSpec((1, 6, D_MODEL), lambda b, j: (b, 0, 0)),
        pl.BlockSpec((1, D_MODEL), const),
        pl.BlockSpec((1, D_MODEL, fc), lambda b, j: (up(j), 0, 0)),
        pl.BlockSpec((1, D_MODEL, fc), lambda b, j: (n_chunk + up(j), 0, 0)),
        pl.BlockSpec((3, fc), lambda b, j: (0, down(j))),
        pl.BlockSpec((1, fc), lambda b, j: (0, down(j))),
        pl.BlockSpec((fc, D_MODEL), lambda b, j: (down(j), 0)),
    ]
    args = [x, mod, g2, wup, wup, cw, cb, wdown]
    if final:
        in_specs.append(pl.BlockSpec((1, D_MODEL), const))
        args.append(final_g)
    return pl.pallas_call(
        functools.partial(_ffn_kernel, seq_len=seq_len, final=final),
        grid=(nb, n_chunk + 1),
        in_specs=in_specs,
        out_specs=pl.BlockSpec((1, n_rows, D_MODEL), lambda b, j: (b, 0, 0)),
        out_shape=jax.ShapeDtypeStruct(x.shape, F32),
        scratch_shapes=[pltpu.VMEM((n_rows, D_MODEL), BF16),
                        pltpu.VMEM((2, n_rows, 2 * fc), F32)],
        compiler_params=_params(("arbitrary", "arbitrary")),
        name="ffn_final" if final else "ffn_%d" % seq_len,
    )(*args)


def _rope_perm():
    quarter = QK_ROPE // 4
    idx = [a * 2 * quarter + j * quarter + i
           for j in range(2) for a in range(2) for i in range(quarter)]
    return np.asarray(idx, np.int32)


@functools.lru_cache(maxsize=None)
def _rope_table(n_tokens):
    t = np.arange(n_tokens)
    half = QK_ROPE // 2
    inv = ROPE_THETA ** (-np.arange(0, half, 2, dtype=np.float64) / half)
    ang = np.concatenate([(t // GRID_W)[:, None] * inv, (t % GRID_W)[:, None] * inv], axis=1)
    cos, sin, zero = np.cos(ang), np.sin(ang), np.zeros_like(ang)
    per_head = lambda a, b: np.tile(np.concatenate([a, b], axis=1), (1, ROPE_TILE // QK_ROPE))
    tab = np.concatenate([per_head(cos, cos), per_head(zero, sin), per_head(-sin, zero)], axis=1)
    return tab.astype(np.float32)


@functools.lru_cache(maxsize=None)
def _chan_dft():
    gw = FNET_W // FNET_GROUPS
    ang = 2.0 * np.pi * ((np.arange(gw)[:, None] * np.arange(gw)[None, :]) % gw) / gw
    eye = np.eye(FNET_GROUPS)
    return np.concatenate([np.kron(eye, np.cos(ang)), np.kron(eye, np.sin(ang))],
                          axis=1) * gw ** -0.5


@functools.lru_cache(maxsize=None)
def _pos_dft(seq_len):
    ang = 2.0 * np.pi * ((np.arange(seq_len)[:, None] * np.arange(seq_len)[None, :])
                         % seq_len) / seq_len
    return np.concatenate([np.cos(ang), -np.sin(ang)], axis=1) * seq_len ** -0.5


def kernel(x, c, ctx, c_ctx, ada_w, ada_b, norm1_g, w_in, q_norm_g, kv_norm_g, w_uq, w_ukv,
           sconv_w, sconv_b, out_norm_g, w_out, norm2_g, w_up, ffconv_w, ffconv_b, w_down,
           final_g):
    nb, seq, _ = x.shape
    n_ctx = ctx.shape[1]

    cond = jnp.concatenate([c, c_ctx[None, :]], axis=0)
    cond = jnp.pad(cond, ((0, MOD_ROWS - cond.shape[0]), (0, 0)))
    mod = _ada_mod(cond, ada_w, ada_b)

    perm = _rope_perm()
    rope_tab = jnp.asarray(_rope_table(seq))
    chan_dft = jnp.asarray(_chan_dft(), F32).astype(BF16)
    pos_dft_lat = jnp.asarray(_pos_dft(seq), F32).astype(BF16)
    pos_dft_ctx = jnp.asarray(_pos_dft(n_ctx), F32).astype(BF16)

    x_lat = x
    x_ctx = ctx
    fc = 256
    row = lambda v: v.reshape(1, -1)
    for l in range(DEPTH):
        last = l == DEPTH - 1
        mod_l = mod[l, :nb].reshape(nb, 6, D_MODEL)
        mod_c = mod[l, nb:nb + 1].reshape(1, 6, D_MODEL)

        k_r = w_in[l][:, Q_LORA + KV_LORA:Q_LORA + KV_LORA + QK_ROPE][:, perm]
        win = jnp.concatenate([w_in[l][:, :Q_LORA + KV_LORA], k_r, k_r,
                               w_in[l][:, Q_LORA + KV_LORA + QK_ROPE:]], axis=1).astype(BF16)
        wq = jnp.concatenate([w_uq[l][:, :, :QK_NOPE].reshape(Q_LORA, -1),
                              w_uq[l][:, :, QK_NOPE:][:, :, perm].reshape(Q_LORA, -1)],
                             axis=1).astype(BF16)
        wkv = jnp.concatenate([w_ukv[l][:, :, :QK_NOPE].reshape(KV_LORA, -1),
                               w_ukv[l][:, :, QK_NOPE:].reshape(KV_LORA, -1)],
                              axis=1).astype(BF16)
        wout = w_out[l].astype(BF16)
        wup = w_up[l].astype(BF16).reshape(D_MODEL, 2 * D_FF // fc, fc).transpose(1, 0, 2)
        wdown = w_down[l].astype(BF16)
        proj = (row(norm1_g[l]), win, row(q_norm_g[l]), row(kv_norm_g[l]), wq, wkv)
        mix = (sconv_w[l], row(sconv_b[l]), row(out_norm_g[l]), wout)
        ffn = (row(norm2_g[l]), wup, ffconv_w[l], row(ffconv_b[l]), wdown)

        heads_l, rest_l = _inproj(x_lat, mod_l, *proj, rope_tab, tr=512)
        heads_c, rest_c = _inproj(x_ctx, mod_c, *proj, None, tr=n_ctx)

        att_l = _attention(heads_l, [heads_c, heads_l], tq=512)
        x_lat = _merge(att_l, rest_l, x_lat, mod_l, chan_dft, pos_dft_lat, *mix, tr=512)
        x_lat = _ffn(x_lat, mod_l, *ffn, seq_len=seq,
                     final_g=row(final_g) if last else None, fc=fc)
        if not last:
            att_c = _attention(heads_c, [heads_c], tq=n_ctx)
            x_ctx = _merge(att_c, rest_c, x_ctx, jnp.broadcast_to(mod_c, (nb, 6, D_MODEL)),
                           chan_dft, pos_dft_ctx, *mix, tr=n_ctx)
            x_ctx = _ffn(x_ctx.reshape(1, nb * n_ctx, D_MODEL), mod_c, *ffn,
                         seq_len=n_ctx, final_g=None, fc=fc).reshape(nb, n_ctx, D_MODEL)
    return x_lat
```

```python
import functools

import jax
import jax.numpy as jnp
import numpy as np
from jax import lax
from jax.experimental import pallas as pl
from jax.experimental.pallas import tpu as pltpu

D_MODEL = 1024
DEPTH = 2
GRID_W = 64
MLA_HEADS = 4
QK_NOPE = 128
QK_ROPE = 64
V_DIM = 128
Q_LORA = 384
KV_LORA = 256
MLA_W = MLA_HEADS * V_DIM
FNET_GROUPS = 4
FNET_W = 256
CONV_W = 256
D_MIX = MLA_W + FNET_W + CONV_W
D_FF = 2816
ROPE_THETA = 10000.0
EPS = 1e-6
SM_SCALE = (QK_NOPE + QK_ROPE) ** -0.5
Q_SCALE = SM_SCALE * float(np.log2(np.e))

LANE = 128
SUBLANE = 8
ROPE_TILE = 2 * QK_ROPE
P_COLS = Q_LORA + KV_LORA + ROPE_TILE + FNET_W + 3 * CONV_W
Q_COLS = MLA_HEADS * QK_NOPE + MLA_HEADS * QK_ROPE
SLAB_QN, SLAB_QR = 0, MLA_HEADS
SLAB_KN = SLAB_QR + MLA_HEADS * QK_ROPE // ROPE_TILE
SLAB_KR = SLAB_KN + MLA_HEADS
SLAB_V = SLAB_KR + 1
N_SLABS = SLAB_V + MLA_HEADS
REST_COLS = FNET_W + 3 * CONV_W
CHAIN_ROWS = 256
UP_ROWS = 1024
MOD_ROWS = 16
VMEM_LIMIT = 62 * 1024 * 1024

BF16 = jnp.bfloat16
F32 = jnp.float32


def _params(sem):
    return pltpu.CompilerParams(dimension_semantics=sem, vmem_limit_bytes=VMEM_LIMIT)


def _rms(x, g):
    return x * lax.rsqrt(jnp.mean(x * x, axis=-1, keepdims=True) + EPS) * g


def _dot(a, b):
    return jnp.dot(a, b, preferred_element_type=F32)


def _roll_rows(z):
    return pltpu.roll(z, 1, axis=0), pltpu.roll(z, z.shape[0] - 1, axis=0)


def _conv3_rows(prev, z, nxt, w_ref, b_ref, seq_len):
    n = z.shape[0]
    w0, w1, w2 = w_ref[0:1, :], w_ref[1:2, :], w_ref[2:3, :]
    out = prev * w0 + z * w1 + nxt * w2 + b_ref[...]
    sub = lax.broadcasted_iota(jnp.int32, (SUBLANE, z.shape[1]), 0)
    pieces, done = [], 0
    for s in range(0, n, seq_len):
        for r, edge, keep_prev, keep_nxt in ((s, 0, False, True),
                                             (s + seq_len - SUBLANE, SUBLANE - 1, True, False)):
            t = slice(r, r + SUBLANE)
            p = prev[t] if keep_prev else jnp.where(sub == edge, 0.0, prev[t])
            q = nxt[t] if keep_nxt else jnp.where(sub == edge, 0.0, nxt[t])
            pieces += [out[done:r], p * w0 + z[t] * w1 + q * w2 + b_ref[...]]
            done = r + SUBLANE
    return jnp.concatenate([x for x in pieces if x.shape[0]], axis=0)


def _ada_kernel(c_ref, w_ref, b_ref, o_ref):
    c = c_ref[...]
    s = (c * jax.nn.sigmoid(c)).astype(BF16)
    o_ref[0] = _dot(s, w_ref[0].astype(BF16)) + b_ref[0]


def _ada_mod(cond, ada_w, ada_b):
    n_col = 6 * D_MODEL
    bc = 1536
    return pl.pallas_call(
        _ada_kernel,
        grid=(DEPTH, n_col // bc),
        in_specs=[
            pl.BlockSpec((MOD_ROWS, D_MODEL), lambda l, j: (0, 0)),
            pl.BlockSpec((1, D_MODEL, bc), lambda l, j: (l, 0, j)),
            pl.BlockSpec((1, 1, bc), lambda l, j: (l, 0, j)),
        ],
        out_specs=pl.BlockSpec((1, MOD_ROWS, bc), lambda l, j: (l, 0, j)),
        out_shape=jax.ShapeDtypeStruct((DEPTH, MOD_ROWS, n_col), F32),
        compiler_params=_params(("arbitrary", "arbitrary")),
        name="ada_mod",
    )(cond, ada_w, ada_b.reshape(DEPTH, 1, n_col))


def _rope_tile(t, tab):
    return (t * tab[:, 0:LANE]
            + pltpu.roll(t, QK_ROPE // 2, axis=1) * tab[:, LANE:2 * LANE]
            + pltpu.roll(t, LANE - QK_ROPE // 2, axis=1) * tab[:, 2 * LANE:3 * LANE])


def _inproj_kernel(*refs, use_rope):
    if use_rope:
        (x_ref, mod_ref, g1_ref, win_ref, qg_ref, kvg_ref, wq_ref, wkv_ref, tab_ref,
         heads_ref, rest_ref) = refs
    else:
        (x_ref, mod_ref, g1_ref, win_ref, qg_ref, kvg_ref, wq_ref, wkv_ref,
         heads_ref, rest_ref) = refs
    n_nope = MLA_HEADS * QK_NOPE
    tr = x_ref.shape[1]
    sub = min(tr, CHAIN_ROWS)
    for c in range(tr // sub):
        rows = slice(c * sub, (c + 1) * sub)
        x = x_ref[0, rows, :]
        h = _rms(x, g1_ref[...]) * (1.0 + mod_ref[0, 1:2, :]) + mod_ref[0, 0:1, :]
        p = _dot(h.astype(BF16), win_ref[...])
        cq = _rms(p[:, :Q_LORA], qg_ref[...]).astype(BF16)
        ckv = _rms(p[:, Q_LORA:Q_LORA + KV_LORA], kvg_ref[...]).astype(BF16)
        q = _dot(cq, wq_ref[...])
        kv = _dot(ckv, wkv_ref[...])
        kr = p[:, Q_LORA + KV_LORA:Q_LORA + KV_LORA + ROPE_TILE]
        qr = [q[:, n_nope + i * LANE:n_nope + (i + 1) * LANE] for i in range(2)]
        if use_rope:
            tab = tab_ref[rows, :]
            kr = _rope_tile(kr, tab)
            qr = [_rope_tile(t, tab) for t in qr]
        slabs = ([q[:, i * LANE:(i + 1) * LANE] * Q_SCALE for i in range(MLA_HEADS)]
                 + [t * Q_SCALE for t in qr]
                 + [kv[:, i * LANE:(i + 1) * LANE] for i in range(MLA_HEADS)]
                 + [kr]
                 + [kv[:, n_nope + i * LANE:n_nope + (i + 1) * LANE] for i in range(MLA_HEADS)])
        for i, slab in enumerate(slabs):
            heads_ref[0, i, rows, :] = slab.astype(BF16)
        rest_ref[0, rows, :] = p[:, Q_LORA + KV_LORA + ROPE_TILE:].astype(BF16)


def _inproj(x, mod, g1, win, qg, kvg, wq, wkv, rope_tab, tr):
    nb, n_rows, _ = x.shape
    use_rope = rope_tab is not None
    const = lambda b, t: (0, 0)
    mod_map = (lambda b, t: (b, 0, 0)) if mod.shape[0] == nb else (lambda b, t: (0, 0, 0))
    in_specs = [
        pl.BlockSpec((1, tr, D_MODEL), lambda b, t: (b, t, 0)),
        pl.BlockSpec((1, 6, D_MODEL), mod_map),
        pl.BlockSpec((1, D_MODEL), const),
        pl.BlockSpec((D_MODEL, P_COLS), const),
        pl.BlockSpec((1, Q_LORA), const),
        pl.BlockSpec((1, KV_LORA), const),
        pl.BlockSpec((Q_LORA, Q_COLS), const),
        pl.BlockSpec((KV_LORA, 2 * MLA_W), const),
    ]
    args = [x, mod, g1, win, qg, kvg, wq, wkv]
    if use_rope:
        in_specs.append(pl.BlockSpec((tr, 3 * LANE), lambda b, t: (t, 0)))
        args.append(rope_tab)
    return pl.pallas_call(
        functools.partial(_inproj_kernel, use_rope=use_rope),
        grid=(nb, n_rows // tr),
        in_specs=in_specs,
        out_specs=[pl.BlockSpec((1, N_SLABS, tr, LANE), lambda b, t: (b, 0, t, 0)),
                   pl.BlockSpec((1, tr, REST_COLS), lambda b, t: (b, t, 0))],
        out_shape=[jax.ShapeDtypeStruct((nb, N_SLABS, n_rows, LANE), BF16),
                   jax.ShapeDtypeStruct((nb, n_rows, REST_COLS), BF16)],
        compiler_params=_params(("arbitrary", "arbitrary")),
        name="inproj_rope" if use_rope else "inproj_ctx",
    )(*args)


def _attn_kernel(*refs, n_src, tq):
    qn_ref, qr_ref = refs[0], refs[1]
    src = refs[2:2 + 3 * n_src]
    o_ref = refs[2 + 3 * n_src]
    k_scr, v_scr = refs[3 + 3 * n_src], refs[4 + 3 * n_src]

    off = 0
    for i in range(n_src):
        kn_ref, kr_ref, v_ref = src[3 * i:3 * i + 3]
        n = kn_ref.shape[2]
        lane = lax.broadcasted_iota(jnp.int32, (n, ROPE_TILE), 1)
        keep = (lane // QK_ROPE) == (pl.program_id(1) % 2)
        k_scr[off:off + n, :QK_NOPE] = kn_ref[0, 0]
        k_scr[off:off + n, QK_NOPE:] = jnp.where(keep, kr_ref[0, 0].astype(F32),
                                                 0.0).astype(BF16)
        v_scr[off:off + n, :] = v_ref[0, 0]
        off += n

    for t in range(qn_ref.shape[2] // tq):
        rows = slice(t * tq, (t + 1) * tq)
        q = jnp.concatenate([qn_ref[0, 0, rows, :], qr_ref[0, 0, rows, :]], axis=1)
        s = lax.dot_general(q, k_scr[...], (((1,), (1,)), ((), ())),
                            preferred_element_type=F32)
        e = jnp.exp2(s - jnp.max(s, axis=1, keepdims=True))
        l = jnp.sum(e, axis=1, keepdims=True)
        o = _dot(e.astype(BF16), v_scr[...])
        o_ref[0, 0, rows, :] = (o / l).astype(BF16)


def _attention(q_heads, kv_list, tq):
    nb, _, n_q, _ = q_heads.shape
    slab = lambda n, first, step=1: pl.BlockSpec(
        (1, 1, n, LANE), lambda b, h: (b, first + h // step if step else first, 0, 0))
    in_specs = [slab(n_q, SLAB_QN), slab(n_q, SLAB_QR, ROPE_TILE // QK_ROPE)]
    args = [q_heads, q_heads]
    n_keys = 0
    for kv in kv_list:
        n = kv.shape[2]
        n_keys += n
        in_specs += [slab(n, SLAB_KN), slab(n, SLAB_KR, 0), slab(n, SLAB_V)]
        args += [kv, kv, kv]
    return pl.pallas_call(
        functools.partial(_attn_kernel, n_src=len(kv_list), tq=tq),
        grid=(nb, MLA_HEADS),
        in_specs=in_specs,
        out_specs=pl.BlockSpec((1, 1, n_q, V_DIM), lambda b, h: (b, h, 0, 0)),
        out_shape=jax.ShapeDtypeStruct((nb, MLA_HEADS, n_q, V_DIM), BF16),
        scratch_shapes=[pltpu.VMEM((n_keys, QK_NOPE + ROPE_TILE), BF16),
                        pltpu.VMEM((n_keys, V_DIM), BF16)],
        compiler_params=_params(("arbitrary", "arbitrary")),
        name="attn_%d" % len(kv_list),
    )(*args)


def _merge_kernel(att_ref, rest_ref, x_ref, mod_ref, chan_ref, dft_ref, cw_ref, cb_ref,
                  gn_ref, wout_ref, o_ref, ab_scr, yc_scr, *, seq_len, tr):
    t = pl.program_id(1)

    @pl.when(t == 0)
    def _():
        ab = _dot(rest_ref[0, :, :FNET_W], chan_ref[...])
        ab_scr[0:seq_len, :] = ab[:, :FNET_W].astype(BF16)
        ab_scr[seq_len:2 * seq_len, :] = ab[:, FNET_W:].astype(BF16)
        bg = rest_ref[0, :, FNET_W:FNET_W + CONV_W].astype(F32)
        cg = rest_ref[0, :, FNET_W + CONV_W:FNET_W + 2 * CONV_W].astype(F32)
        xv = rest_ref[0, :, FNET_W + 2 * CONV_W:].astype(F32)
        z = cg * xv
        prev, nxt = _roll_rows(z)
        yc = bg * _conv3_rows(prev, z, nxt, cw_ref, cb_ref, seq_len)
        yc_scr[...] = _rms(yc, gn_ref[:, MLA_W + FNET_W:]).astype(BF16)

    sub = min(tr, CHAIN_ROWS)
    for c in range(tr // sub):
        rows = slice(c * sub, (c + 1) * sub)
        yf = _dot(dft_ref[rows, :], ab_scr[...])
        yf = _rms(yf, gn_ref[:, MLA_W:MLA_W + FNET_W]).astype(BF16)
        att = jnp.concatenate([att_ref[0, i, rows, :] for i in range(MLA_HEADS)], axis=1)
        ya = _rms(att.astype(F32), gn_ref[:, :MLA_W]).astype(BF16)
        r0 = pl.multiple_of(t * tr + c * sub, sub)
        y = jnp.concatenate([ya, yf, yc_scr[pl.ds(r0, sub), :]], axis=1)
        o_ref[0, rows, :] = x_ref[0, rows, :] + mod_ref[0, 2:3, :] * _dot(y, wout_ref[...])


def _merge(att, rest, x, mod, chan_dft, pos_dft, cw, cb, gn, wout, tr):
    nb, seq_len, _ = x.shape
    const = lambda b, t: (0, 0)
    return pl.pallas_call(
        functools.partial(_merge_kernel, seq_len=seq_len, tr=tr),
        grid=(nb, seq_len // tr),
        in_specs=[
            pl.BlockSpec((1, MLA_HEADS, tr, V_DIM), lambda b, t: (b, 0, t, 0)),
            pl.BlockSpec((1, seq_len, REST_COLS), lambda b, t: (b, 0, 0)),
            pl.BlockSpec((1, tr, D_MODEL), lambda b, t: (b, t, 0)),
            pl.BlockSpec((1, 6, D_MODEL), lambda b, t: (b, 0, 0)),
            pl.BlockSpec((FNET_W, 2 * FNET_W), const),
            pl.BlockSpec((tr, 2 * seq_len), lambda b, t: (t, 0)),
            pl.BlockSpec((3, CONV_W), const),
            pl.BlockSpec((1, CONV_W), const),
            pl.BlockSpec((1, D_MIX), const),
            pl.BlockSpec((D_MIX, D_MODEL), const),
        ],
        out_specs=pl.BlockSpec((1, tr, D_MODEL), lambda b, t: (b, t, 0)),
        out_shape=jax.ShapeDtypeStruct(x.shape, F32),
        scratch_shapes=[pltpu.VMEM((2 * seq_len, FNET_W), BF16),
                        pltpu.VMEM((seq_len, CONV_W), BF16)],
        compiler_params=_params(("arbitrary", "arbitrary")),
        name="merge_%d" % seq_len,
    )(att, rest, x, mod, chan_dft, pos_dft, cw, cb, gn, wout)


def _ffn_kernel(*refs, seq_len, final, n_chunk):
    if final:
        (x_ref, mod_ref, g2_ref, wg_ref, wv_ref, cw_ref, cb_ref, wd_ref, fg_ref,
         o_ref, h_scr, u_even, u_odd) = refs
    else:
        (x_ref, mod_ref, g2_ref, wg_ref, wv_ref, cw_ref, cb_ref, wd_ref,
         o_ref, h_scr, u_even, u_odd) = refs
    j = pl.program_id(1)
    fc = wg_ref.shape[2]

    def up_project(u_ref):
        w = jnp.concatenate([wg_ref[0].astype(BF16), wv_ref[0].astype(BF16)], axis=1)
        for r in range(0, h_scr.shape[0], UP_ROWS):
            u_ref[r:r + UP_ROWS, :] = _dot(h_scr[r:r + UP_ROWS, :], w)

    def accumulate_down(u_ref):
        z = u_ref[:, :fc]
        prev, nxt = _roll_rows(z)
        gate = _conv3_rows(prev, z, nxt, cw_ref.at[0], cb_ref.at[0], seq_len)
        act = gate * jax.nn.sigmoid(gate) * u_ref[:, fc:]
        o_ref[0] += _dot(act.astype(BF16), wd_ref[0].astype(BF16))

    @pl.when(j == 0)
    def _():
        h = _rms(x_ref[0], g2_ref[...]) * (1.0 + mod_ref[0, 4:5, :]) + mod_ref[0, 3:4, :]
        h_scr[...] = h.astype(BF16)
        o_ref[0] = jnp.zeros(o_ref.shape[1:], F32)
        up_project(u_even)

    steady = (j > 0) & (j < n_chunk)

    @pl.when(steady & (j % 2 == 1))
    def _():
        up_project(u_odd)
        accumulate_down(u_even)

    @pl.when(steady & (j % 2 == 0))
    def _():
        up_project(u_even)
        accumulate_down(u_odd)

    @pl.when(j == n_chunk)
    def _():
        accumulate_down(u_even if (n_chunk - 1) % 2 == 0 else u_odd)
        y = x_ref[0] + mod_ref[0, 5:6, :] * o_ref[0]
        if final:
            y = _rms(y, fg_ref[...])
        o_ref[0] = y


def _ffn(x, mod, g2, layer, w_up, cw, cb, w_down, seq_len, final_g, fc):
    nb, n_rows, _ = x.shape
    n_chunk = D_FF // fc
    final = final_g is not None
    const = lambda b, j: (0, 0)
    up = lambda j: jnp.minimum(j, n_chunk - 1)
    down = lambda j: jnp.maximum(j - 1, 0)
    in_specs = [
        pl.BlockSpec((1, n_rows, D_MODEL), lambda b, j: (b, 0, 0)),
        pl.BlockSpec((1, 6, D_MODEL), lambda b, j: (b, 0, 0)),
        pl.BlockSpec((1, D_MODEL), const),
        pl.BlockSpec((1, D_MODEL, fc), lambda b, j: (layer, 0, up(j))),
        pl.BlockSpec((1, D_MODEL, fc), lambda b, j: (layer, 0, n_chunk + up(j))),
        pl.BlockSpec((1, 3, fc), lambda b, j: (layer, 0, down(j))),
        pl.BlockSpec((1, 1, fc), lambda b, j: (layer, 0, down(j))),
        pl.BlockSpec((1, fc, D_MODEL), lambda b, j: (layer, down(j), 0)),
    ]
    args = [x, mod, g2, w_up, w_up, cw, cb.reshape(DEPTH, 1, D_FF), w_down]
    if final:
        in_specs.append(pl.BlockSpec((1, D_MODEL), const))
        args.append(final_g)
    return pl.pallas_call(
        functools.partial(_ffn_kernel, seq_len=seq_len, final=final, n_chunk=n_chunk),
        grid=(nb, n_chunk + 1),
        in_specs=in_specs,
        out_specs=pl.BlockSpec((1, n_rows, D_MODEL), lambda b, j: (b, 0, 0)),
        out_shape=jax.ShapeDtypeStruct(x.shape, F32),
        scratch_shapes=[pltpu.VMEM((n_rows, D_MODEL), BF16),
                        pltpu.VMEM((n_rows, 2 * fc), F32),
                        pltpu.VMEM((n_rows, 2 * fc), F32)],
        compiler_params=_params(("arbitrary", "arbitrary")),
        name="ffn_final" if final else "ffn_%d" % seq_len,
    )(*args)


def _rope_perm():
    quarter = QK_ROPE // 4
    idx = [a * 2 * quarter + j * quarter + i
           for j in range(2) for a in range(2) for i in range(quarter)]
    return np.asarray(idx, np.int32)


@functools.lru_cache(maxsize=None)
def _rope_table(n_tokens):
    t = np.arange(n_tokens)
    half = QK_ROPE // 2
    inv = ROPE_THETA ** (-np.arange(0, half, 2, dtype=np.float64) / half)
    ang = np.concatenate([(t // GRID_W)[:, None] * inv, (t % GRID_W)[:, None] * inv], axis=1)
    cos, sin, zero = np.cos(ang), np.sin(ang), np.zeros_like(ang)
    per_head = lambda a, b: np.tile(np.concatenate([a, b], axis=1), (1, ROPE_TILE // QK_ROPE))
    tab = np.concatenate([per_head(cos, cos), per_head(zero, sin), per_head(-sin, zero)], axis=1)
    return tab.astype(np.float32)


@functools.lru_cache(maxsize=None)
def _chan_dft():
    gw = FNET_W // FNET_GROUPS
    ang = 2.0 * np.pi * ((np.arange(gw)[:, None] * np.arange(gw)[None, :]) % gw) / gw
    eye = np.eye(FNET_GROUPS)
    return np.concatenate([np.kron(eye, np.cos(ang)), np.kron(eye, np.sin(ang))],
                          axis=1) * gw ** -0.5


@functools.lru_cache(maxsize=None)
def _pos_dft(seq_len):
    ang = 2.0 * np.pi * ((np.arange(seq_len)[:, None] * np.arange(seq_len)[None, :])
                         % seq_len) / seq_len
    return np.concatenate([np.cos(ang), -np.sin(ang)], axis=1) * seq_len ** -0.5


def kernel(x, c, ctx, c_ctx, ada_w, ada_b, norm1_g, w_in, q_norm_g, kv_norm_g, w_uq, w_ukv,
           sconv_w, sconv_b, out_norm_g, w_out, norm2_g, w_up, ffconv_w, ffconv_b, w_down,
           final_g):
    nb, seq, _ = x.shape
    n_ctx = ctx.shape[1]

    cond = jnp.concatenate([c, c_ctx[None, :]], axis=0)
    cond = jnp.pad(cond, ((0, MOD_ROWS - cond.shape[0]), (0, 0)))
    mod = _ada_mod(cond, ada_w, ada_b)

    perm = _rope_perm()
    rope_tab = jnp.asarray(_rope_table(seq))
    chan_dft = jnp.asarray(_chan_dft(), F32).astype(BF16)
    pos_dft_lat = jnp.asarray(_pos_dft(seq), F32).astype(BF16)
    pos_dft_ctx = jnp.asarray(_pos_dft(n_ctx), F32).astype(BF16)

    x_lat = x
    x_ctx = ctx
    fc = 256
    row = lambda v: v.reshape(1, -1)
    for l in range(DEPTH):
        last = l == DEPTH - 1
        mod_l = mod[l, :nb].reshape(nb, 6, D_MODEL)
        mod_c = mod[l, nb:nb + 1].reshape(1, 6, D_MODEL)

        k_r = w_in[l][:, Q_LORA + KV_LORA:Q_LORA + KV_LORA + QK_ROPE][:, perm]
        win = jnp.concatenate([w_in[l][:, :Q_LORA + KV_LORA], k_r, k_r,
                               w_in[l][:, Q_LORA + KV_LORA + QK_ROPE:]], axis=1).astype(BF16)
        wq = jnp.concatenate([w_uq[l][:, :, :QK_NOPE].reshape(Q_LORA, -1),
                              w_uq[l][:, :, QK_NOPE:][:, :, perm].reshape(Q_LORA, -1)],
                             axis=1).astype(BF16)
        wkv = jnp.concatenate([w_ukv[l][:, :, :QK_NOPE].reshape(KV_LORA, -1),
                               w_ukv[l][:, :, QK_NOPE:].reshape(KV_LORA, -1)],
                              axis=1).astype(BF16)
        wout = w_out[l].astype(BF16)
        proj = (row(norm1_g[l]), win, row(q_norm_g[l]), row(kv_norm_g[l]), wq, wkv)
        mix = (sconv_w[l], row(sconv_b[l]), row(out_norm_g[l]), wout)
        ffn = (row(norm2_g[l]), l, w_up, ffconv_w, ffconv_b, w_down)

        heads_l, rest_l = _inproj(x_lat, mod_l, *proj, rope_tab, tr=512)
        heads_c, rest_c = _inproj(x_ctx, mod_c, *proj, None, tr=n_ctx)

        att_l = _attention(heads_l, [heads_c, heads_l], tq=512)
        x_lat = _merge(att_l, rest_l, x_lat, mod_l, chan_dft, pos_dft_lat, *mix, tr=512)
        x_lat = _ffn(x_lat, mod_l, *ffn, seq_len=seq,
                     final_g=row(final_g) if last else None, fc=fc)
        if not last:
            att_c = _attention(heads_c, [heads_c], tq=n_ctx)
            x_ctx = _merge(att_c, rest_c, x_ctx, jnp.broadcast_to(mod_c, (nb, 6, D_MODEL)),
                           chan_dft, pos_dft_ctx, *mix, tr=n_ctx)
            x_ctx = _ffn(x_ctx.reshape(1, nb * n_ctx, D_MODEL), mod_c, *ffn,
                         seq_len=n_ctx, final_g=None, fc=fc).reshape(nb, n_ctx, D_MODEL)
    return x_lat
```

```python
import functools

import jax
import jax.numpy as jnp
import numpy as np
from jax import lax
from jax.experimental import pallas as pl
from jax.experimental.pallas import tpu as pltpu

D_MODEL = 1024
DEPTH = 2
GRID_W = 64
MLA_HEADS = 4
QK_NOPE = 128
QK_ROPE = 64
V_DIM = 128
Q_LORA = 384
KV_LORA = 256
MLA_W = MLA_HEADS * V_DIM
FNET_GROUPS = 4
FNET_W = 256
CONV_W = 256
D_MIX = MLA_W + FNET_W + CONV_W
D_FF = 2816
ROPE_THETA = 10000.0
EPS = 1e-6
SM_SCALE = (QK_NOPE + QK_ROPE) ** -0.5
Q_SCALE = SM_SCALE * float(np.log2(np.e))

LANE = 128
SUBLANE = 8
ROPE_TILE = 2 * QK_ROPE
P_COLS = Q_LORA + KV_LORA + ROPE_TILE + FNET_W + 3 * CONV_W
Q_COLS = MLA_HEADS * QK_NOPE + MLA_HEADS * QK_ROPE
SLAB_QN, SLAB_QR = 0, MLA_HEADS
SLAB_KN = SLAB_QR + MLA_HEADS * QK_ROPE // ROPE_TILE
SLAB_KR = SLAB_KN + MLA_HEADS
SLAB_V = SLAB_KR + 1
N_SLABS = SLAB_V + MLA_HEADS
REST_COLS = FNET_W + 3 * CONV_W
CHAIN_ROWS = 256
UP_ROWS = 128
MOD_ROWS = 16
VMEM_LIMIT = 62 * 1024 * 1024

BF16 = jnp.bfloat16
F32 = jnp.float32


def _params(sem):
    return pltpu.CompilerParams(dimension_semantics=sem, vmem_limit_bytes=VMEM_LIMIT)


def _rms(x, g):
    return x * lax.rsqrt(jnp.mean(x * x, axis=-1, keepdims=True) + EPS) * g


def _dot(a, b):
    return jnp.dot(a, b, preferred_element_type=F32)


def _roll_rows(z):
    return pltpu.roll(z, 1, axis=0), pltpu.roll(z, z.shape[0] - 1, axis=0)


def _conv3_rows(prev, z, nxt, w_ref, b_ref, seq_len):
    n = z.shape[0]
    w0, w1, w2 = w_ref[0:1, :], w_ref[1:2, :], w_ref[2:3, :]
    out = prev * w0 + z * w1 + nxt * w2 + b_ref[...]
    sub = lax.broadcasted_iota(jnp.int32, (SUBLANE, z.shape[1]), 0)
    pieces, done = [], 0
    for s in range(0, n, seq_len):
        for r, edge, keep_prev, keep_nxt in ((s, 0, False, True),
                                             (s + seq_len - SUBLANE, SUBLANE - 1, True, False)):
            t = slice(r, r + SUBLANE)
            p = prev[t] if keep_prev else jnp.where(sub == edge, 0.0, prev[t])
            q = nxt[t] if keep_nxt else jnp.where(sub == edge, 0.0, nxt[t])
            pieces += [out[done:r], p * w0 + z[t] * w1 + q * w2 + b_ref[...]]
            done = r + SUBLANE
    return jnp.concatenate([x for x in pieces if x.shape[0]], axis=0)


def _ada_kernel(c_ref, w_ref, b_ref, o_ref):
    c = c_ref[...]
    s = (c * jax.nn.sigmoid(c)).astype(BF16)
    o_ref[0] = _dot(s, w_ref[0].astype(BF16)) + b_ref[0]


def _ada_mod(cond, ada_w, ada_b):
    n_col = 6 * D_MODEL
    bc = 1536
    return pl.pallas_call(
        _ada_kernel,
        grid=(DEPTH, n_col // bc),
        in_specs=[
            pl.BlockSpec((MOD_ROWS, D_MODEL), lambda l, j: (0, 0)),
            pl.BlockSpec((1, D_MODEL, bc), lambda l, j: (l, 0, j)),
            pl.BlockSpec((1, 1, bc), lambda l, j: (l, 0, j)),
        ],
        out_specs=pl.BlockSpec((1, MOD_ROWS, bc), lambda l, j: (l, 0, j)),
        out_shape=jax.ShapeDtypeStruct((DEPTH, MOD_ROWS, n_col), F32),
        compiler_params=_params(("arbitrary", "arbitrary")),
        name="ada_mod",
    )(cond, ada_w, ada_b.reshape(DEPTH, 1, n_col))


def _rope_tile(t, tab):
    return (t * tab[:, 0:LANE]
            + pltpu.roll(t, QK_ROPE // 2, axis=1) * tab[:, LANE:2 * LANE]
            + pltpu.roll(t, LANE - QK_ROPE // 2, axis=1) * tab[:, 2 * LANE:3 * LANE])


def _inproj_kernel(*refs, use_rope):
    if use_rope:
        (x_ref, mod_ref, g1_ref, win_ref, qg_ref, kvg_ref, wq_ref, wkv_ref, tab_ref,
         heads_ref, rest_ref) = refs
    else:
        (x_ref, mod_ref, g1_ref, win_ref, qg_ref, kvg_ref, wq_ref, wkv_ref,
         heads_ref, rest_ref) = refs
    n_nope = MLA_HEADS * QK_NOPE
    tr = x_ref.shape[1]
    sub = min(tr, CHAIN_ROWS)
    for c in range(tr // sub):
        rows = slice(c * sub, (c + 1) * sub)
        x = x_ref[0, rows, :]
        h = _rms(x, g1_ref[...]) * (1.0 + mod_ref[0, 1:2, :]) + mod_ref[0, 0:1, :]
        p = _dot(h.astype(BF16), win_ref[...])
        cq = _rms(p[:, :Q_LORA], qg_ref[...]).astype(BF16)
        ckv = _rms(p[:, Q_LORA:Q_LORA + KV_LORA], kvg_ref[...]).astype(BF16)
        q = _dot(cq, wq_ref[...])
        kv = _dot(ckv, wkv_ref[...])
        kr = p[:, Q_LORA + KV_LORA:Q_LORA + KV_LORA + ROPE_TILE]
        qr = [q[:, n_nope + i * LANE:n_nope + (i + 1) * LANE] for i in range(2)]
        if use_rope:
            tab = tab_ref[rows, :]
            kr = _rope_tile(kr, tab)
            qr = [_rope_tile(t, tab) for t in qr]
        slabs = ([q[:, i * LANE:(i + 1) * LANE] * Q_SCALE for i in range(MLA_HEADS)]
                 + [t * Q_SCALE for t in qr]
                 + [kv[:, i * LANE:(i + 1) * LANE] for i in range(MLA_HEADS)]
                 + [kr]
                 + [kv[:, n_nope + i * LANE:n_nope + (i + 1) * LANE] for i in range(MLA_HEADS)])
        for i, slab in enumerate(slabs):
            heads_ref[0, i, rows, :] = slab.astype(BF16)
        rest_ref[0, rows, :] = p[:, Q_LORA + KV_LORA + ROPE_TILE:].astype(BF16)


def _inproj(x, mod, g1, win, qg, kvg, wq, wkv, rope_tab, tr):
    nb, n_rows, _ = x.shape
    use_rope = rope_tab is not None
    const = lambda b, t: (0, 0)
    mod_map = (lambda b, t: (b, 0, 0)) if mod.shape[0] == nb else (lambda b, t: (0, 0, 0))
    in_specs = [
        pl.BlockSpec((1, tr, D_MODEL), lambda b, t: (b, t, 0)),
        pl.BlockSpec((1, 6, D_MODEL), mod_map),
        pl.BlockSpec((1, D_MODEL), const),
        pl.BlockSpec((D_MODEL, P_COLS), const),
        pl.BlockSpec((1, Q_LORA), const),
        pl.BlockSpec((1, KV_LORA), const),
        pl.BlockSpec((Q_LORA, Q_COLS), const),
        pl.BlockSpec((KV_LORA, 2 * MLA_W), const),
    ]
    args = [x, mod, g1, win, qg, kvg, wq, wkv]
    if use_rope:
        in_specs.append(pl.BlockSpec((tr, 3 * LANE), lambda b, t: (t, 0)))
        args.append(rope_tab)
    return pl.pallas_call(
        functools.partial(_inproj_kernel, use_rope=use_rope),
        grid=(nb, n_rows // tr),
        in_specs=in_specs,
        out_specs=[pl.BlockSpec((1, N_SLABS, tr, LANE), lambda b, t: (b, 0, t, 0)),
                   pl.BlockSpec((1, tr, REST_COLS), lambda b, t: (b, t, 0))],
        out_shape=[jax.ShapeDtypeStruct((nb, N_SLABS, n_rows, LANE), BF16),
                   jax.ShapeDtypeStruct((nb, n_rows, REST_COLS), BF16)],
        compiler_params=_params(("arbitrary", "arbitrary")),
        name="inproj_rope" if use_rope else "inproj_ctx",
    )(*args)


def _attn_kernel(*refs, n_src, tq):
    qn_ref, qr_ref = refs[0], refs[1]
    src = refs[2:2 + 3 * n_src]
    o_ref = refs[2 + 3 * n_src]
    k_scr, v_scr = refs[3 + 3 * n_src], refs[4 + 3 * n_src]

    off = 0
    for i in range(n_src):
        kn_ref, kr_ref, v_ref = src[3 * i:3 * i + 3]
        n = kn_ref.shape[2]
        lane = lax.broadcasted_iota(jnp.int32, (n, ROPE_TILE), 1)
        keep = (lane // QK_ROPE) == (pl.program_id(1) % 2)
        k_scr[off:off + n, :QK_NOPE] = kn_ref[0, 0]
        k_scr[off:off + n, QK_NOPE:] = jnp.where(keep, kr_ref[0, 0].astype(F32),
                                                 0.0).astype(BF16)
        v_scr[off:off + n, :] = v_ref[0, 0]
        off += n

    for t in range(qn_ref.shape[2] // tq):
        rows = slice(t * tq, (t + 1) * tq)
        q = jnp.concatenate([qn_ref[0, 0, rows, :], qr_ref[0, 0, rows, :]], axis=1)
        s = lax.dot_general(q, k_scr[...], (((1,), (1,)), ((), ())),
                            preferred_element_type=F32)
        e = jnp.exp2(s - jnp.max(s, axis=1, keepdims=True))
        l = jnp.sum(e, axis=1, keepdims=True)
        o = _dot(e.astype(BF16), v_scr[...])
        o_ref[0, 0, rows, :] = (o / l).astype(BF16)


def _attention(q_heads, kv_list, tq):
    nb, _, n_q, _ = q_heads.shape
    slab = lambda n, first, step=1: pl.BlockSpec(
        (1, 1, n, LANE), lambda b, h: (b, first + h // step if step else first, 0, 0))
    in_specs = [slab(n_q, SLAB_QN), slab(n_q, SLAB_QR, ROPE_TILE // QK_ROPE)]
    args = [q_heads, q_heads]
    n_keys = 0
    for kv in kv_list:
        n = kv.shape[2]
        n_keys += n
        in_specs += [slab(n, SLAB_KN), slab(n, SLAB_KR, 0), slab(n, SLAB_V)]
        args += [kv, kv, kv]
    return pl.pallas_call(
        functools.partial(_attn_kernel, n_src=len(kv_list), tq=tq),
        grid=(nb, MLA_HEADS),
        in_specs=in_specs,
        out_specs=pl.BlockSpec((1, 1, n_q, V_DIM), lambda b, h: (b, h, 0, 0)),
        out_shape=jax.ShapeDtypeStruct((nb, MLA_HEADS, n_q, V_DIM), BF16),
        scratch_shapes=[pltpu.VMEM((n_keys, QK_NOPE + ROPE_TILE), BF16),
                        pltpu.VMEM((n_keys, V_DIM), BF16)],
        compiler_params=_params(("arbitrary", "arbitrary")),
        name="attn_%d" % len(kv_list),
    )(*args)


def _merge_kernel(att_ref, rest_ref, x_ref, mod_ref, chan_ref, dft_ref, cw_ref, cb_ref,
                  gn_ref, wout_ref, o_ref, ab_scr, yc_scr, *, seq_len, tr):
    t = pl.program_id(1)

    @pl.when(t == 0)
    def _():
        ab = _dot(rest_ref[0, :, :FNET_W], chan_ref[...])
        ab_scr[0:seq_len, :] = ab[:, :FNET_W].astype(BF16)
        ab_scr[seq_len:2 * seq_len, :] = ab[:, FNET_W:].astype(BF16)
        bg = rest_ref[0, :, FNET_W:FNET_W + CONV_W].astype(F32)
        cg = rest_ref[0, :, FNET_W + CONV_W:FNET_W + 2 * CONV_W].astype(F32)
        xv = rest_ref[0, :, FNET_W + 2 * CONV_W:].astype(F32)
        z = cg * xv
        prev, nxt = _roll_rows(z)
        yc = bg * _conv3_rows(prev, z, nxt, cw_ref, cb_ref, seq_len)
        yc_scr[...] = _rms(yc, gn_ref[:, MLA_W + FNET_W:]).astype(BF16)

    sub = min(tr, CHAIN_ROWS)
    for c in range(tr // sub):
        rows = slice(c * sub, (c + 1) * sub)
        yf = _dot(dft_ref[rows, :], ab_scr[...])
        yf = _rms(yf, gn_ref[:, MLA_W:MLA_W + FNET_W]).astype(BF16)
        att = jnp.concatenate([att_ref[0, i, rows, :] for i in range(MLA_HEADS)], axis=1)
        ya = _rms(att.astype(F32), gn_ref[:, :MLA_W]).astype(BF16)
        r0 = pl.multiple_of(t * tr + c * sub, sub)
        y = jnp.concatenate([ya, yf, yc_scr[pl.ds(r0, sub), :]], axis=1)
        o_ref[0, rows, :] = x_ref[0, rows, :] + mod_ref[0, 2:3, :] * _dot(y, wout_ref[...])


def _merge(att, rest, x, mod, chan_dft, pos_dft, cw, cb, gn, wout, tr):
    nb, seq_len, _ = x.shape
    const = lambda b, t: (0, 0)
    return pl.pallas_call(
        functools.partial(_merge_kernel, seq_len=seq_len, tr=tr),
        grid=(nb, seq_len // tr),
        in_specs=[
            pl.BlockSpec((1, MLA_HEADS, tr, V_DIM), lambda b, t: (b, 0, t, 0)),
            pl.BlockSpec((1, seq_len, REST_COLS), lambda b, t: (b, 0, 0)),
            pl.BlockSpec((1, tr, D_MODEL), lambda b, t: (b, t, 0)),
            pl.BlockSpec((1, 6, D_MODEL), lambda b, t: (b, 0, 0)),
            pl.BlockSpec((FNET_W, 2 * FNET_W), const),
            pl.BlockSpec((tr, 2 * seq_len), lambda b, t: (t, 0)),
            pl.BlockSpec((3, CONV_W), const),
            pl.BlockSpec((1, CONV_W), const),
            pl.BlockSpec((1, D_MIX), const),
            pl.BlockSpec((D_MIX, D_MODEL), const),
        ],
        out_specs=pl.BlockSpec((1, tr, D_MODEL), lambda b, t: (b, t, 0)),
        out_shape=jax.ShapeDtypeStruct(x.shape, F32),
        scratch_shapes=[pltpu.VMEM((2 * seq_len, FNET_W), BF16),
                        pltpu.VMEM((seq_len, CONV_W), BF16)],
        compiler_params=_params(("arbitrary", "arbitrary")),
        name="merge_%d" % seq_len,
    )(att, rest, x, mod, chan_dft, pos_dft, cw, cb, gn, wout)


def _ffn_kernel(*refs, seq_len, final, n_chunk):
    if final:
        (x_ref, mod_ref, g2_ref, wg_ref, wv_ref, cw_ref, cb_ref, wd_ref, fg_ref,
         o_ref, h_scr, act_even, act_odd) = refs
    else:
        (x_ref, mod_ref, g2_ref, wg_ref, wv_ref, cw_ref, cb_ref, wd_ref,
         o_ref, h_scr, act_even, act_odd) = refs
    j = pl.program_id(1)
    fc = wg_ref.shape[2]

    def up_activate(act_ref):
        w = jnp.concatenate([wg_ref[0].astype(BF16), wv_ref[0].astype(BF16)], axis=1)
        u = jnp.concatenate([_dot(h_scr[r:r + UP_ROWS, :], w)
                             for r in range(0, h_scr.shape[0], UP_ROWS)], axis=0)
        z = u[:, :fc]
        prev, nxt = _roll_rows(z)
        gate = _conv3_rows(prev, z, nxt, cw_ref.at[0], cb_ref.at[0], seq_len)
        act_ref[...] = (gate * jax.nn.sigmoid(gate) * u[:, fc:]).astype(BF16)

    def accumulate_down(act_ref):
        o_ref[0] += _dot(act_ref[...], wd_ref[0].astype(BF16))

    @pl.when(j == 0)
    def _():
        h = _rms(x_ref[0], g2_ref[...]) * (1.0 + mod_ref[0, 4:5, :]) + mod_ref[0, 3:4, :]
        h_scr[...] = h.astype(BF16)
        o_ref[0] = jnp.zeros(o_ref.shape[1:], F32)
        up_activate(act_even)

    steady = (j > 0) & (j < n_chunk)

    @pl.when(steady & (j % 2 == 1))
    def _():
        up_activate(act_odd)
        accumulate_down(act_even)

    @pl.when(steady & (j % 2 == 0))
    def _():
        up_activate(act_even)
        accumulate_down(act_odd)

    @pl.when(j == n_chunk)
    def _():
        accumulate_down(act_even if (n_chunk - 1) % 2 == 0 else act_odd)
        y = x_ref[0] + mod_ref[0, 5:6, :] * o_ref[0]
        if final:
            y = _rms(y, fg_ref[...])
        o_ref[0] = y


def _ffn(x, mod, g2, layer, w_up, cw, cb, w_down, seq_len, final_g, fc):
    nb, n_rows, _ = x.shape
    n_chunk = D_FF // fc
    final = final_g is not None
    const = lambda b, j: (0, 0)
    up = lambda j: jnp.minimum(j, n_chunk - 1)
    down = lambda j: jnp.maximum(j - 1, 0)
    in_specs = [
        pl.BlockSpec((1, n_rows, D_MODEL), lambda b, j: (b, 0, 0)),
        pl.BlockSpec((1, 6, D_MODEL), lambda b, j: (b, 0, 0)),
        pl.BlockSpec((1, D_MODEL), const),
        pl.BlockSpec((1, D_MODEL, fc), lambda b, j: (layer, 0, up(j))),
        pl.BlockSpec((1, D_MODEL, fc), lambda b, j: (layer, 0, n_chunk + up(j))),
        pl.BlockSpec((1, 3, fc), lambda b, j: (layer, 0, up(j))),
        pl.BlockSpec((1, 1, fc), lambda b, j: (layer, 0, up(j))),
        pl.BlockSpec((1, fc, D_MODEL), lambda b, j: (layer, down(j), 0)),
    ]
    args = [x, mod, g2, w_up, w_up, cw, cb.reshape(DEPTH, 1, D_FF), w_down]
    if final:
        in_specs.append(pl.BlockSpec((1, D_MODEL), const))
        args.append(final_g)
    return pl.pallas_call(
        functools.partial(_ffn_kernel, seq_len=seq_len, final=final, n_chunk=n_chunk),
        grid=(nb, n_chunk + 1),
        in_specs=in_specs,
        out_specs=pl.BlockSpec((1, n_rows, D_MODEL), lambda b, j: (b, 0, 0)),
        out_shape=jax.ShapeDtypeStruct(x.shape, F32),
        scratch_shapes=[pltpu.VMEM((n_rows, D_MODEL), BF16),
                        pltpu.VMEM((n_rows, fc), BF16),
                        pltpu.VMEM((n_rows, fc), BF16)],
        compiler_params=_params(("arbitrary", "arbitrary")),
        name="ffn_final" if final else "ffn_%d" % seq_len,
    )(*args)


def _rope_perm():
    quarter = QK_ROPE // 4
    idx = [a * 2 * quarter + j * quarter + i
           for j in range(2) for a in range(2) for i in range(quarter)]
    return np.asarray(idx, np.int32)


@functools.lru_cache(maxsize=None)
def _rope_table(n_tokens):
    t = np.arange(n_tokens)
    half = QK_ROPE // 2
    inv = ROPE_THETA ** (-np.arange(0, half, 2, dtype=np.float64) / half)
    ang = np.concatenate([(t // GRID_W)[:, None] * inv, (t % GRID_W)[:, None] * inv], axis=1)
    cos, sin, zero = np.cos(ang), np.sin(ang), np.zeros_like(ang)
    per_head = lambda a, b: np.tile(np.concatenate([a, b], axis=1), (1, ROPE_TILE // QK_ROPE))
    tab = np.concatenate([per_head(cos, cos), per_head(zero, sin), per_head(-sin, zero)], axis=1)
    return tab.astype(np.float32)


@functools.lru_cache(maxsize=None)
def _chan_dft():
    gw = FNET_W // FNET_GROUPS
    ang = 2.0 * np.pi * ((np.arange(gw)[:, None] * np.arange(gw)[None, :]) % gw) / gw
    eye = np.eye(FNET_GROUPS)
    return np.concatenate([np.kron(eye, np.cos(ang)), np.kron(eye, np.sin(ang))],
                          axis=1) * gw ** -0.5


@functools.lru_cache(maxsize=None)
def _pos_dft(seq_len):
    ang = 2.0 * np.pi * ((np.arange(seq_len)[:, None] * np.arange(seq_len)[None, :])
                         % seq_len) / seq_len
    return np.concatenate([np.cos(ang), -np.sin(ang)], axis=1) * seq_len ** -0.5


def kernel(x, c, ctx, c_ctx, ada_w, ada_b, norm1_g, w_in, q_norm_g, kv_norm_g, w_uq, w_ukv,
           sconv_w, sconv_b, out_norm_g, w_out, norm2_g, w_up, ffconv_w, ffconv_b, w_down,
           final_g):
    nb, seq, _ = x.shape
    n_ctx = ctx.shape[1]

    cond = jnp.concatenate([c, c_ctx[None, :]], axis=0)
    cond = jnp.pad(cond, ((0, MOD_ROWS - cond.shape[0]), (0, 0)))
    mod = _ada_mod(cond, ada_w, ada_b)

    perm = _rope_perm()
    rope_tab = jnp.asarray(_rope_table(seq))
    chan_dft = jnp.asarray(_chan_dft(), F32).astype(BF16)
    pos_dft_lat = jnp.asarray(_pos_dft(seq), F32).astype(BF16)
    pos_dft_ctx = jnp.asarray(_pos_dft(n_ctx), F32).astype(BF16)

    x_lat = x
    x_ctx = ctx
    fc = 256
    row = lambda v: v.reshape(1, -1)
    for l in range(DEPTH):
        last = l == DEPTH - 1
        mod_l = mod[l, :nb].reshape(nb, 6, D_MODEL)
        mod_c = mod[l, nb:nb + 1].reshape(1, 6, D_MODEL)

        k_r = w_in[l][:, Q_LORA + KV_LORA:Q_LORA + KV_LORA + QK_ROPE][:, perm]
        win = jnp.concatenate([w_in[l][:, :Q_LORA + KV_LORA], k_r, k_r,
                               w_in[l][:, Q_LORA + KV_LORA + QK_ROPE:]], axis=1).astype(BF16)
        wq = jnp.concatenate([w_uq[l][:, :, :QK_NOPE].reshape(Q_LORA, -1),
                              w_uq[l][:, :, QK_NOPE:][:, :, perm].reshape(Q_LORA, -1)],
                             axis=1).astype(BF16)
        wkv = jnp.concatenate([w_ukv[l][:, :, :QK_NOPE].reshape(KV_LORA, -1),
                               w_ukv[l][:, :, QK_NOPE:].reshape(KV_LORA, -1)],
                              axis=1).astype(BF16)
        wout = w_out[l].astype(BF16)
        proj = (row(norm1_g[l]), win, row(q_norm_g[l]), row(kv_norm_g[l]), wq, wkv)
        mix = (sconv_w[l], row(sconv_b[l]), row(out_norm_g[l]), wout)
        ffn = (row(norm2_g[l]), l, w_up, ffconv_w, ffconv_b, w_down)

        heads_l, rest_l = _inproj(x_lat, mod_l, *proj, rope_tab, tr=512)
        heads_c, rest_c = _inproj(x_ctx, mod_c, *proj, None, tr=n_ctx)

        att_l = _attention(heads_l, [heads_c, heads_l], tq=512)
        x_lat = _merge(att_l, rest_l, x_lat, mod_l, chan_dft, pos_dft_lat, *mix, tr=512)
        x_lat = _ffn(x_lat, mod_l, *ffn, seq_len=seq,
                     final_g=row(final_g) if last else None, fc=fc)
        if not last:
            att_c = _attention(heads_c, [heads_c], tq=n_ctx)
            x_ctx = _merge(att_c, rest_c, x_ctx, jnp.broadcast_to(mod_c, (nb, 6, D_MODEL)),
                           chan_dft, pos_dft_ctx, *mix, tr=n_ctx)
            x_ctx = _ffn(x_ctx.reshape(1, nb * n_ctx, D_MODEL), mod_c, *ffn,
                         seq_len=n_ctx, final_g=None, fc=fc).reshape(nb, n_ctx, D_MODEL)
    return x_lat
```

```python
import functools

import jax
import jax.numpy as jnp
import numpy as np
from jax import lax
from jax.experimental import pallas as pl
from jax.experimental.pallas import tpu as pltpu

D_MODEL = 1024
DEPTH = 2
GRID_W = 64
MLA_HEADS = 4
QK_NOPE = 128
QK_ROPE = 64
V_DIM = 128
Q_LORA = 384
KV_LORA = 256
MLA_W = MLA_HEADS * V_DIM
FNET_GROUPS = 4
FNET_W = 256
CONV_W = 256
D_MIX = MLA_W + FNET_W + CONV_W
D_FF = 2816
ROPE_THETA = 10000.0
EPS = 1e-6
SM_SCALE = (QK_NOPE + QK_ROPE) ** -0.5
Q_SCALE = SM_SCALE * float(np.log2(np.e))

LANE = 128
SUBLANE = 8
ROPE_TILE = 2 * QK_ROPE
P_COLS = Q_LORA + KV_LORA + ROPE_TILE + FNET_W + 3 * CONV_W
Q_COLS = MLA_HEADS * QK_NOPE + MLA_HEADS * QK_ROPE
SLAB_QN, SLAB_QR = 0, MLA_HEADS
SLAB_KN = SLAB_QR + MLA_HEADS * QK_ROPE // ROPE_TILE
SLAB_KR = SLAB_KN + MLA_HEADS
SLAB_V = SLAB_KR + 1
N_SLABS = SLAB_V + MLA_HEADS
REST_COLS = FNET_W + 3 * CONV_W
CHAIN_ROWS = 256
UP_ROWS = 512
MOD_ROWS = 16
VMEM_LIMIT = 62 * 1024 * 1024

BF16 = jnp.bfloat16
F32 = jnp.float32


def _params(sem):
    return pltpu.CompilerParams(dimension_semantics=sem, vmem_limit_bytes=VMEM_LIMIT)


def _rms(x, g):
    return x * lax.rsqrt(jnp.mean(x * x, axis=-1, keepdims=True) + EPS) * g


def _dot(a, b):
    return jnp.dot(a, b, preferred_element_type=F32)


def _roll_rows(z):
    return pltpu.roll(z, 1, axis=0), pltpu.roll(z, z.shape[0] - 1, axis=0)


def _conv3_rows(prev, z, nxt, w_ref, b_ref, seq_len):
    n = z.shape[0]
    w0, w1, w2 = w_ref[0:1, :], w_ref[1:2, :], w_ref[2:3, :]
    out = prev * w0 + z * w1 + nxt * w2 + b_ref[...]
    sub = lax.broadcasted_iota(jnp.int32, (SUBLANE, z.shape[1]), 0)
    pieces, done = [], 0
    for s in range(0, n, seq_len):
        for r, edge, keep_prev, keep_nxt in ((s, 0, False, True),
                                             (s + seq_len - SUBLANE, SUBLANE - 1, True, False)):
            t = slice(r, r + SUBLANE)
            p = prev[t] if keep_prev else jnp.where(sub == edge, 0.0, prev[t])
            q = nxt[t] if keep_nxt else jnp.where(sub == edge, 0.0, nxt[t])
            pieces += [out[done:r], p * w0 + z[t] * w1 + q * w2 + b_ref[...]]
            done = r + SUBLANE
    return jnp.concatenate([x for x in pieces if x.shape[0]], axis=0)


def _ada_kernel(c_ref, w_ref, b_ref, o_ref):
    c = c_ref[...]
    s = (c * jax.nn.sigmoid(c)).astype(BF16)
    o_ref[0] = _dot(s, w_ref[0].astype(BF16)) + b_ref[0]


def _ada_mod(cond, ada_w, ada_b):
    n_col = 6 * D_MODEL
    bc = 1536
    return pl.pallas_call(
        _ada_kernel,
        grid=(DEPTH, n_col // bc),
        in_specs=[
            pl.BlockSpec((MOD_ROWS, D_MODEL), lambda l, j: (0, 0)),
            pl.BlockSpec((1, D_MODEL, bc), lambda l, j: (l, 0, j)),
            pl.BlockSpec((1, 1, bc), lambda l, j: (l, 0, j)),
        ],
        out_specs=pl.BlockSpec((1, MOD_ROWS, bc), lambda l, j: (l, 0, j)),
        out_shape=jax.ShapeDtypeStruct((DEPTH, MOD_ROWS, n_col), F32),
        compiler_params=_params(("arbitrary", "arbitrary")),
        name="ada_mod",
    )(cond, ada_w, ada_b.reshape(DEPTH, 1, n_col))


def _rope_tile(t, tab):
    return (t * tab[:, 0:LANE]
            + pltpu.roll(t, QK_ROPE // 2, axis=1) * tab[:, LANE:2 * LANE]
            + pltpu.roll(t, LANE - QK_ROPE // 2, axis=1) * tab[:, 2 * LANE:3 * LANE])


def _inproj_kernel(*refs, use_rope):
    if use_rope:
        (x_ref, mod_ref, g1_ref, win_ref, qg_ref, kvg_ref, wq_ref, wkv_ref, tab_ref,
         heads_ref, rest_ref) = refs
    else:
        (x_ref, mod_ref, g1_ref, win_ref, qg_ref, kvg_ref, wq_ref, wkv_ref,
         heads_ref, rest_ref) = refs
    n_nope = MLA_HEADS * QK_NOPE
    tr = x_ref.shape[1]
    sub = min(tr, CHAIN_ROWS)
    for c in range(tr // sub):
        rows = slice(c * sub, (c + 1) * sub)
        x = x_ref[0, rows, :]
        h = _rms(x, g1_ref[...]) * (1.0 + mod_ref[0, 1:2, :]) + mod_ref[0, 0:1, :]
        p = _dot(h.astype(BF16), win_ref[...])
        cq = _rms(p[:, :Q_LORA], qg_ref[...]).astype(BF16)
        ckv = _rms(p[:, Q_LORA:Q_LORA + KV_LORA], kvg_ref[...]).astype(BF16)
        q = _dot(cq, wq_ref[...])
        kv = _dot(ckv, wkv_ref[...])
        kr = p[:, Q_LORA + KV_LORA:Q_LORA + KV_LORA + ROPE_TILE]
        qr = [q[:, n_nope + i * LANE:n_nope + (i + 1) * LANE] for i in range(2)]
        if use_rope:
            tab = tab_ref[rows, :]
            kr = _rope_tile(kr, tab)
            qr = [_rope_tile(t, tab) for t in qr]
        slabs = ([q[:, i * LANE:(i + 1) * LANE] * Q_SCALE for i in range(MLA_HEADS)]
                 + [t * Q_SCALE for t in qr]
                 + [kv[:, i * LANE:(i + 1) * LANE] for i in range(MLA_HEADS)]
                 + [kr]
                 + [kv[:, n_nope + i * LANE:n_nope + (i + 1) * LANE] for i in range(MLA_HEADS)])
        for i, slab in enumerate(slabs):
            heads_ref[0, i, rows, :] = slab.astype(BF16)
        rest_ref[0, rows, :] = p[:, Q_LORA + KV_LORA + ROPE_TILE:].astype(BF16)


def _inproj(x, mod, g1, win, qg, kvg, wq, wkv, rope_tab, tr):
    nb, n_rows, _ = x.shape
    use_rope = rope_tab is not None
    const = lambda b, t: (0, 0)
    mod_map = (lambda b, t: (b, 0, 0)) if mod.shape[0] == nb else (lambda b, t: (0, 0, 0))
    in_specs = [
        pl.BlockSpec((1, tr, D_MODEL), lambda b, t: (b, t, 0)),
        pl.BlockSpec((1, 6, D_MODEL), mod_map),
        pl.BlockSpec((1, D_MODEL), const),
        pl.BlockSpec((D_MODEL, P_COLS), const),
        pl.BlockSpec((1, Q_LORA), const),
        pl.BlockSpec((1, KV_LORA), const),
        pl.BlockSpec((Q_LORA, Q_COLS), const),
        pl.BlockSpec((KV_LORA, 2 * MLA_W), const),
    ]
    args = [x, mod, g1, win, qg, kvg, wq, wkv]
    if use_rope:
        in_specs.append(pl.BlockSpec((tr, 3 * LANE), lambda b, t: (t, 0)))
        args.append(rope_tab)
    return pl.pallas_call(
        functools.partial(_inproj_kernel, use_rope=use_rope),
        grid=(nb, n_rows // tr),
        in_specs=in_specs,
        out_specs=[pl.BlockSpec((1, N_SLABS, tr, LANE), lambda b, t: (b, 0, t, 0)),
                   pl.BlockSpec((1, tr, REST_COLS), lambda b, t: (b, t, 0))],
        out_shape=[jax.ShapeDtypeStruct((nb, N_SLABS, n_rows, LANE), BF16),
                   jax.ShapeDtypeStruct((nb, n_rows, REST_COLS), BF16)],
        compiler_params=_params(("arbitrary", "arbitrary")),
        name="inproj_rope" if use_rope else "inproj_ctx",
    )(*args)


def _attn_kernel(*refs, n_src, tq):
    qn_ref, qr_ref = refs[0], refs[1]
    src = refs[2:2 + 3 * n_src]
    o_ref = refs[2 + 3 * n_src]
    k_scr, v_scr = refs[3 + 3 * n_src], refs[4 + 3 * n_src]

    off = 0
    for i in range(n_src):
        kn_ref, kr_ref, v_ref = src[3 * i:3 * i + 3]
        n = kn_ref.shape[2]
        lane = lax.broadcasted_iota(jnp.int32, (n, ROPE_TILE), 1)
        keep = (lane // QK_ROPE) == (pl.program_id(1) % 2)
        k_scr[off:off + n, :QK_NOPE] = kn_ref[0, 0]
        k_scr[off:off + n, QK_NOPE:] = jnp.where(keep, kr_ref[0, 0].astype(F32),
                                                 0.0).astype(BF16)
        v_scr[off:off + n, :] = v_ref[0, 0]
        off += n

    for t in range(qn_ref.shape[2] // tq):
        rows = slice(t * tq, (t + 1) * tq)
        q = jnp.concatenate([qn_ref[0, 0, rows, :], qr_ref[0, 0, rows, :]], axis=1)
        s = lax.dot_general(q, k_scr[...], (((1,), (1,)), ((), ())),
                            preferred_element_type=F32)
        e = jnp.exp2(s - jnp.max(s, axis=1, keepdims=True))
        l = jnp.sum(e, axis=1, keepdims=True)
        o = _dot(e.astype(BF16), v_scr[...])
        o_ref[0, 0, rows, :] = (o / l).astype(BF16)


def _attention(q_heads, kv_list, tq):
    nb, _, n_q, _ = q_heads.shape
    slab = lambda n, first, step=1: pl.BlockSpec(
        (1, 1, n, LANE), lambda b, h: (b, first + h // step if step else first, 0, 0))
    in_specs = [slab(n_q, SLAB_QN), slab(n_q, SLAB_QR, ROPE_TILE // QK_ROPE)]
    args = [q_heads, q_heads]
    n_keys = 0
    for kv in kv_list:
        n = kv.shape[2]
        n_keys += n
        in_specs += [slab(n, SLAB_KN), slab(n, SLAB_KR, 0), slab(n, SLAB_V)]
        args += [kv, kv, kv]
    return pl.pallas_call(
        functools.partial(_attn_kernel, n_src=len(kv_list), tq=tq),
        grid=(nb, MLA_HEADS),
        in_specs=in_specs,
        out_specs=pl.BlockSpec((1, 1, n_q, V_DIM), lambda b, h: (b, h, 0, 0)),
        out_shape=jax.ShapeDtypeStruct((nb, MLA_HEADS, n_q, V_DIM), BF16),
        scratch_shapes=[pltpu.VMEM((n_keys, QK_NOPE + ROPE_TILE), BF16),
                        pltpu.VMEM((n_keys, V_DIM), BF16)],
        compiler_params=_params(("arbitrary", "arbitrary")),
        name="attn_%d" % len(kv_list),
    )(*args)


def _merge_kernel(att_ref, rest_ref, x_ref, mod_ref, chan_ref, dft_ref, cw_ref, cb_ref,
                  gn_ref, wout_ref, o_ref, ab_scr, yc_scr, *, seq_len, tr):
    t = pl.program_id(1)

    @pl.when(t == 0)
    def _():
        ab = _dot(rest_ref[0, :, :FNET_W], chan_ref[...])
        ab_scr[0:seq_len, :] = ab[:, :FNET_W].astype(BF16)
        ab_scr[seq_len:2 * seq_len, :] = ab[:, FNET_W:].astype(BF16)
        bg = rest_ref[0, :, FNET_W:FNET_W + CONV_W].astype(F32)
        cg = rest_ref[0, :, FNET_W + CONV_W:FNET_W + 2 * CONV_W].astype(F32)
        xv = rest_ref[0, :, FNET_W + 2 * CONV_W:].astype(F32)
        z = cg * xv
        prev, nxt = _roll_rows(z)
        yc = bg * _conv3_rows(prev, z, nxt, cw_ref, cb_ref, seq_len)
        yc_scr[...] = _rms(yc, gn_ref[:, MLA_W + FNET_W:]).astype(BF16)

    sub = min(tr, CHAIN_ROWS)
    for c in range(tr // sub):
        rows = slice(c * sub, (c + 1) * sub)
        yf = _dot(dft_ref[rows, :], ab_scr[...])
        yf = _rms(yf, gn_ref[:, MLA_W:MLA_W + FNET_W]).astype(BF16)
        att = jnp.concatenate([att_ref[0, i, rows, :] for i in range(MLA_HEADS)], axis=1)
        ya = _rms(att.astype(F32), gn_ref[:, :MLA_W]).astype(BF16)
        r0 = pl.multiple_of(t * tr + c * sub, sub)
        y = jnp.concatenate([ya, yf, yc_scr[pl.ds(r0, sub), :]], axis=1)
        o_ref[0, rows, :] = x_ref[0, rows, :] + mod_ref[0, 2:3, :] * _dot(y, wout_ref[...])


def _merge(att, rest, x, mod, chan_dft, pos_dft, cw, cb, gn, wout, tr):
    nb, seq_len, _ = x.shape
    const = lambda b, t: (0, 0)
    return pl.pallas_call(
        functools.partial(_merge_kernel, seq_len=seq_len, tr=tr),
        grid=(nb, seq_len // tr),
        in_specs=[
            pl.BlockSpec((1, MLA_HEADS, tr, V_DIM), lambda b, t: (b, 0, t, 0)),
            pl.BlockSpec((1, seq_len, REST_COLS), lambda b, t: (b, 0, 0)),
            pl.BlockSpec((1, tr, D_MODEL), lambda b, t: (b, t, 0)),
            pl.BlockSpec((1, 6, D_MODEL), lambda b, t: (b, 0, 0)),
            pl.BlockSpec((FNET_W, 2 * FNET_W), const),
            pl.BlockSpec((tr, 2 * seq_len), lambda b, t: (t, 0)),
            pl.BlockSpec((3, CONV_W), const),
            pl.BlockSpec((1, CONV_W), const),
            pl.BlockSpec((1, D_MIX), const),
            pl.BlockSpec((D_MIX, D_MODEL), const),
        ],
        out_specs=pl.BlockSpec((1, tr, D_MODEL), lambda b, t: (b, t, 0)),
        out_shape=jax.ShapeDtypeStruct(x.shape, F32),
        scratch_shapes=[pltpu.VMEM((2 * seq_len, FNET_W), BF16),
                        pltpu.VMEM((seq_len, CONV_W), BF16)],
        compiler_params=_params(("arbitrary", "arbitrary")),
        name="merge_%d" % seq_len,
    )(att, rest, x, mod, chan_dft, pos_dft, cw, cb, gn, wout)


def _ffn_kernel(*refs, seq_len, final, n_chunk):
    if final:
        (x_ref, mod_ref, g2_ref, wg_ref, wv_ref, cw_ref, cb_ref, wd_ref, fg_ref,
         o_ref, h_scr, act_even, act_odd) = refs
    else:
        (x_ref, mod_ref, g2_ref, wg_ref, wv_ref, cw_ref, cb_ref, wd_ref,
         o_ref, h_scr, act_even, act_odd) = refs
    j = pl.program_id(1)
    fc = wg_ref.shape[2]

    def up_activate(act_ref):
        w = jnp.concatenate([wg_ref[0].astype(BF16), wv_ref[0].astype(BF16)], axis=1)
        u = jnp.concatenate([_dot(h_scr[r:r + UP_ROWS, :], w)
                             for r in range(0, h_scr.shape[0], UP_ROWS)], axis=0)
        z = u[:, :fc]
        prev, nxt = _roll_rows(z)
        gate = _conv3_rows(prev, z, nxt, cw_ref.at[0], cb_ref.at[0], seq_len)
        act_ref[...] = (gate * jax.nn.sigmoid(gate) * u[:, fc:]).astype(BF16)

    def accumulate_down(act_ref):
        o_ref[0] += _dot(act_ref[...], wd_ref[0].astype(BF16))

    @pl.when(j == 0)
    def _():
        h = _rms(x_ref[0], g2_ref[...]) * (1.0 + mod_ref[0, 4:5, :]) + mod_ref[0, 3:4, :]
        h_scr[...] = h.astype(BF16)
        o_ref[0] = jnp.zeros(o_ref.shape[1:], F32)
        up_activate(act_even)

    steady = (j > 0) & (j < n_chunk)

    @pl.when(steady & (j % 2 == 1))
    def _():
        up_activate(act_odd)
        accumulate_down(act_even)

    @pl.when(steady & (j % 2 == 0))
    def _():
        up_activate(act_even)
        accumulate_down(act_odd)

    @pl.when(j == n_chunk)
    def _():
        accumulate_down(act_even if (n_chunk - 1) % 2 == 0 else act_odd)
        y = x_ref[0] + mod_ref[0, 5:6, :] * o_ref[0]
        if final:
            y = _rms(y, fg_ref[...])
        o_ref[0] = y


def _ffn(x, mod, g2, layer, w_up, cw, cb, w_down, seq_len, final_g, fc):
    nb, n_rows, _ = x.shape
    n_chunk = D_FF // fc
    final = final_g is not None
    const = lambda b, j: (0, 0)
    up = lambda j: jnp.minimum(j, n_chunk - 1)
    down = lambda j: jnp.maximum(j - 1, 0)
    in_specs = [
        pl.BlockSpec((1, n_rows, D_MODEL), lambda b, j: (b, 0, 0)),
        pl.BlockSpec((1, 6, D_MODEL), lambda b, j: (b, 0, 0)),
        pl.BlockSpec((1, D_MODEL), const),
        pl.BlockSpec((1, D_MODEL, fc), lambda b, j: (layer, 0, up(j))),
        pl.BlockSpec((1, D_MODEL, fc), lambda b, j: (layer, 0, n_chunk + up(j))),
        pl.BlockSpec((1, 3, fc), lambda b, j: (layer, 0, up(j))),
        pl.BlockSpec((1, 1, fc), lambda b, j: (layer, 0, up(j))),
        pl.BlockSpec((1, fc, D_MODEL), lambda b, j: (layer, down(j), 0)),
    ]
    args = [x, mod, g2, w_up, w_up, cw, cb.reshape(DEPTH, 1, D_FF), w_down]
    if final:
        in_specs.append(pl.BlockSpec((1, D_MODEL), const))
        args.append(final_g)
    return pl.pallas_call(
        functools.partial(_ffn_kernel, seq_len=seq_len, final=final, n_chunk=n_chunk),
        grid=(nb, n_chunk + 1),
        in_specs=in_specs,
        out_specs=pl.BlockSpec((1, n_rows, D_MODEL), lambda b, j: (b, 0, 0)),
        out_shape=jax.ShapeDtypeStruct(x.shape, F32),
        scratch_shapes=[pltpu.VMEM((n_rows, D_MODEL), BF16),
                        pltpu.VMEM((n_rows, fc), BF16),
                        pltpu.VMEM((n_rows, fc), BF16)],
        compiler_params=_params(("arbitrary", "arbitrary")),
        name="ffn_final" if final else "ffn_%d" % seq_len,
    )(*args)


def _rope_perm():
    quarter = QK_ROPE // 4
    idx = [a * 2 * quarter + j * quarter + i
           for j in range(2) for a in range(2) for i in range(quarter)]
    return np.asarray(idx, np.int32)


@functools.lru_cache(maxsize=None)
def _rope_table(n_tokens):
    t = np.arange(n_tokens)
    half = QK_ROPE // 2
    inv = ROPE_THETA ** (-np.arange(0, half, 2, dtype=np.float64) / half)
    ang = np.concatenate([(t // GRID_W)[:, None] * inv, (t % GRID_W)[:, None] * inv], axis=1)
    cos, sin, zero = np.cos(ang), np.sin(ang), np.zeros_like(ang)
    per_head = lambda a, b: np.tile(np.concatenate([a, b], axis=1), (1, ROPE_TILE // QK_ROPE))
    tab = np.concatenate([per_head(cos, cos), per_head(zero, sin), per_head(-sin, zero)], axis=1)
    return tab.astype(np.float32)


@functools.lru_cache(maxsize=None)
def _chan_dft():
    gw = FNET_W // FNET_GROUPS
    ang = 2.0 * np.pi * ((np.arange(gw)[:, None] * np.arange(gw)[None, :]) % gw) / gw
    eye = np.eye(FNET_GROUPS)
    return np.concatenate([np.kron(eye, np.cos(ang)), np.kron(eye, np.sin(ang))],
                          axis=1) * gw ** -0.5


@functools.lru_cache(maxsize=None)
def _pos_dft(seq_len):
    ang = 2.0 * np.pi * ((np.arange(seq_len)[:, None] * np.arange(seq_len)[None, :])
                         % seq_len) / seq_len
    return np.concatenate([np.cos(ang), -np.sin(ang)], axis=1) * seq_len ** -0.5


def kernel(x, c, ctx, c_ctx, ada_w, ada_b, norm1_g, w_in, q_norm_g, kv_norm_g, w_uq, w_ukv,
           sconv_w, sconv_b, out_norm_g, w_out, norm2_g, w_up, ffconv_w, ffconv_b, w_down,
           final_g):
    nb, seq, _ = x.shape
    n_ctx = ctx.shape[1]

    cond = jnp.concatenate([c, c_ctx[None, :]], axis=0)
    cond = jnp.pad(cond, ((0, MOD_ROWS - cond.shape[0]), (0, 0)))
    mod = _ada_mod(cond, ada_w, ada_b)

    perm = _rope_perm()
    rope_tab = jnp.asarray(_rope_table(seq))
    chan_dft = jnp.asarray(_chan_dft(), F32).astype(BF16)
    pos_dft_lat = jnp.asarray(_pos_dft(seq), F32).astype(BF16)
    pos_dft_ctx = jnp.asarray(_pos_dft(n_ctx), F32).astype(BF16)

    x_lat = x
    x_ctx = ctx
    fc = 256
    row = lambda v: v.reshape(1, -1)
    for l in range(DEPTH):
        last = l == DEPTH - 1
        mod_l = mod[l, :nb].reshape(nb, 6, D_MODEL)
        mod_c = mod[l, nb:nb + 1].reshape(1, 6, D_MODEL)

        k_r = w_in[l][:, Q_LORA + KV_LORA:Q_LORA + KV_LORA + QK_ROPE][:, perm]
        win = jnp.concatenate([w_in[l][:, :Q_LORA + KV_LORA], k_r, k_r,
                               w_in[l][:, Q_LORA + KV_LORA + QK_ROPE:]], axis=1).astype(BF16)
        wq = jnp.concatenate([w_uq[l][:, :, :QK_NOPE].reshape(Q_LORA, -1),
                              w_uq[l][:, :, QK_NOPE:][:, :, perm].reshape(Q_LORA, -1)],
                             axis=1).astype(BF16)
        wkv = jnp.concatenate([w_ukv[l][:, :, :QK_NOPE].reshape(KV_LORA, -1),
                               w_ukv[l][:, :, QK_NOPE:].reshape(KV_LORA, -1)],
                              axis=1).astype(BF16)
        wout = w_out[l].astype(BF16)
        proj = (row(norm1_g[l]), win, row(q_norm_g[l]), row(kv_norm_g[l]), wq, wkv)
        mix = (sconv_w[l], row(sconv_b[l]), row(out_norm_g[l]), wout)
        ffn = (row(norm2_g[l]), l, w_up, ffconv_w, ffconv_b, w_down)

        heads_l, rest_l = _inproj(x_lat, mod_l, *proj, rope_tab, tr=512)
        heads_c, rest_c = _inproj(x_ctx, mod_c, *proj, None, tr=n_ctx)

        att_l = _attention(heads_l, [heads_c, heads_l], tq=512)
        x_lat = _merge(att_l, rest_l, x_lat, mod_l, chan_dft, pos_dft_lat, *mix, tr=512)
        x_lat = _ffn(x_lat, mod_l, *ffn, seq_len=seq,
                     final_g=row(final_g) if last else None, fc=fc)
        if not last:
            att_c = _attention(heads_c, [heads_c], tq=n_ctx)
            x_ctx = _merge(att_c, rest_c, x_ctx, jnp.broadcast_to(mod_c, (nb, 6, D_MODEL)),
                           chan_dft, pos_dft_ctx, *mix, tr=n_ctx)
            x_ctx = _ffn(x_ctx.reshape(1, nb * n_ctx, D_MODEL), mod_c, *ffn,
                         seq_len=n_ctx, final_g=None, fc=fc).reshape(nb, n_ctx, D_MODEL)
    return x_lat
```

```python
import functools

import jax
import jax.numpy as jnp
import numpy as np
from jax import lax
from jax.experimental import pallas as pl
from jax.experimental.pallas import tpu as pltpu

D_MODEL = 1024
DEPTH = 2
GRID_W = 64
MLA_HEADS = 4
QK_NOPE = 128
QK_ROPE = 64
V_DIM = 128
Q_LORA = 384
KV_LORA = 256
MLA_W = MLA_HEADS * V_DIM
FNET_GROUPS = 4
FNET_W = 256
CONV_W = 256
D_MIX = MLA_W + FNET_W + CONV_W
D_FF = 2816
ROPE_THETA = 10000.0
EPS = 1e-6
SM_SCALE = (QK_NOPE + QK_ROPE) ** -0.5
Q_SCALE = SM_SCALE * float(np.log2(np.e))

LANE = 128
SUBLANE = 8
ROPE_TILE = 2 * QK_ROPE
P_COLS = Q_LORA + KV_LORA + ROPE_TILE + FNET_W + 3 * CONV_W
Q_COLS = MLA_HEADS * QK_NOPE + MLA_HEADS * QK_ROPE
SLAB_QN, SLAB_QR = 0, MLA_HEADS
SLAB_KN = SLAB_QR + MLA_HEADS * QK_ROPE // ROPE_TILE
SLAB_KR = SLAB_KN + MLA_HEADS
SLAB_V = SLAB_KR + 1
N_SLABS = SLAB_V + MLA_HEADS
REST_COLS = FNET_W + 3 * CONV_W
CHAIN_ROWS = 512
UP_ROWS = 1024
MOD_ROWS = 16
VMEM_LIMIT = 62 * 1024 * 1024

BF16 = jnp.bfloat16
F32 = jnp.float32


def _params(sem):
    return pltpu.CompilerParams(dimension_semantics=sem, vmem_limit_bytes=VMEM_LIMIT)


def _rms(x, g):
    return x * lax.rsqrt(jnp.mean(x * x, axis=-1, keepdims=True) + EPS) * g


def _dot(a, b):
    return jnp.dot(a, b, preferred_element_type=F32)


def _roll_rows(z):
    return pltpu.roll(z, 1, axis=0), pltpu.roll(z, z.shape[0] - 1, axis=0)


def _conv3_rows(prev, z, nxt, w_ref, b_ref, seq_len):
    n = z.shape[0]
    w0, w1, w2 = w_ref[0:1, :], w_ref[1:2, :], w_ref[2:3, :]
    out = prev * w0 + z * w1 + nxt * w2 + b_ref[...]
    sub = lax.broadcasted_iota(jnp.int32, (SUBLANE, z.shape[1]), 0)
    pieces, done = [], 0
    for s in range(0, n, seq_len):
        for r, edge, keep_prev, keep_nxt in ((s, 0, False, True),
                                             (s + seq_len - SUBLANE, SUBLANE - 1, True, False)):
            t = slice(r, r + SUBLANE)
            p = prev[t] if keep_prev else jnp.where(sub == edge, 0.0, prev[t])
            q = nxt[t] if keep_nxt else jnp.where(sub == edge, 0.0, nxt[t])
            pieces += [out[done:r], p * w0 + z[t] * w1 + q * w2 + b_ref[...]]
            done = r + SUBLANE
    return jnp.concatenate([x for x in pieces if x.shape[0]], axis=0)


def _ada_kernel(c_ref, w_ref, b_ref, o_ref):
    c = c_ref[...]
    s = (c * jax.nn.sigmoid(c)).astype(BF16)
    o_ref[0] = _dot(s, w_ref[0].astype(BF16)) + b_ref[0]


def _ada_mod(cond, ada_w, ada_b):
    n_col = 6 * D_MODEL
    bc = 1536
    return pl.pallas_call(
        _ada_kernel,
        grid=(DEPTH, n_col // bc),
        in_specs=[
            pl.BlockSpec((MOD_ROWS, D_MODEL), lambda l, j: (0, 0)),
            pl.BlockSpec((1, D_MODEL, bc), lambda l, j: (l, 0, j)),
            pl.BlockSpec((1, 1, bc), lambda l, j: (l, 0, j)),
        ],
        out_specs=pl.BlockSpec((1, MOD_ROWS, bc), lambda l, j: (l, 0, j)),
        out_shape=jax.ShapeDtypeStruct((DEPTH, MOD_ROWS, n_col), F32),
        compiler_params=_params(("arbitrary", "arbitrary")),
        name="ada_mod",
    )(cond, ada_w, ada_b.reshape(DEPTH, 1, n_col))


def _rope_tile(t, tab):
    return (t * tab[:, 0:LANE]
            + pltpu.roll(t, QK_ROPE // 2, axis=1) * tab[:, LANE:2 * LANE]
            + pltpu.roll(t, LANE - QK_ROPE // 2, axis=1) * tab[:, 2 * LANE:3 * LANE])


def _inproj_kernel(*refs, use_rope):
    if use_rope:
        (x_ref, mod_ref, g1_ref, win_ref, qg_ref, kvg_ref, wq_ref, wkv_ref, tab_ref,
         heads_ref, rest_ref) = refs
    else:
        (x_ref, mod_ref, g1_ref, win_ref, qg_ref, kvg_ref, wq_ref, wkv_ref,
         heads_ref, rest_ref) = refs
    n_nope = MLA_HEADS * QK_NOPE
    tr = x_ref.shape[1]
    sub = min(tr, CHAIN_ROWS)
    for c in range(tr // sub):
        rows = slice(c * sub, (c + 1) * sub)
        x = x_ref[0, rows, :]
        h = _rms(x, g1_ref[0]) * (1.0 + mod_ref[0, 0, 1:2, :]) + mod_ref[0, 0, 0:1, :]
        p = _dot(h.astype(BF16), win_ref[0])
        cq = _rms(p[:, :Q_LORA], qg_ref[0]).astype(BF16)
        ckv = _rms(p[:, Q_LORA:Q_LORA + KV_LORA], kvg_ref[0]).astype(BF16)
        q = _dot(cq, wq_ref[0])
        kv = _dot(ckv, wkv_ref[0])
        kr = p[:, Q_LORA + KV_LORA:Q_LORA + KV_LORA + ROPE_TILE]
        qr = [q[:, n_nope + i * LANE:n_nope + (i + 1) * LANE] for i in range(2)]
        if use_rope:
            tab = tab_ref[rows, :]
            kr = _rope_tile(kr, tab)
            qr = [_rope_tile(t, tab) for t in qr]
        slabs = ([q[:, i * LANE:(i + 1) * LANE] * Q_SCALE for i in range(MLA_HEADS)]
                 + [t * Q_SCALE for t in qr]
                 + [kv[:, i * LANE:(i + 1) * LANE] for i in range(MLA_HEADS)]
                 + [kr]
                 + [kv[:, n_nope + i * LANE:n_nope + (i + 1) * LANE] for i in range(MLA_HEADS)])
        for i, slab in enumerate(slabs):
            heads_ref[0, i, rows, :] = slab.astype(BF16)
        rest_ref[0, rows, :] = p[:, Q_LORA + KV_LORA + ROPE_TILE:].astype(BF16)


def _layer_spec(arr, layer):
    zeros = (0,) * (arr.ndim - 1)
    return pl.BlockSpec((1,) + arr.shape[1:], lambda *_: (layer,) + zeros)


def _mod_spec(layer, mod_row):
    return pl.BlockSpec((1, 1, 6, D_MODEL), lambda b, t: (layer, mod_row(b), 0, 0))


def _inproj(x, mod, mod_row, layer, g1, win, qg, kvg, wq, wkv, rope_tab, tr):
    nb, n_rows, _ = x.shape
    use_rope = rope_tab is not None
    params = (g1, win, qg, kvg, wq, wkv)
    in_specs = [pl.BlockSpec((1, tr, D_MODEL), lambda b, t: (b, t, 0)), _mod_spec(layer, mod_row)]
    in_specs += [_layer_spec(p, layer) for p in params]
    args = [x, mod, *params]
    if use_rope:
        in_specs.append(pl.BlockSpec((tr, 3 * LANE), lambda b, t: (t, 0)))
        args.append(rope_tab)
    return pl.pallas_call(
        functools.partial(_inproj_kernel, use_rope=use_rope),
        grid=(nb, n_rows // tr),
        in_specs=in_specs,
        out_specs=[pl.BlockSpec((1, N_SLABS, tr, LANE), lambda b, t: (b, 0, t, 0)),
                   pl.BlockSpec((1, tr, REST_COLS), lambda b, t: (b, t, 0))],
        out_shape=[jax.ShapeDtypeStruct((nb, N_SLABS, n_rows, LANE), BF16),
                   jax.ShapeDtypeStruct((nb, n_rows, REST_COLS), BF16)],
        compiler_params=_params(("arbitrary", "arbitrary")),
        name="inproj_rope" if use_rope else "inproj_ctx",
    )(*args)


def _attn_kernel(*refs, n_src, tq):
    qn_ref, qr_ref = refs[0], refs[1]
    src = refs[2:2 + 3 * n_src]
    o_ref = refs[2 + 3 * n_src]
    k_scr, v_scr = refs[3 + 3 * n_src], refs[4 + 3 * n_src]

    off = 0
    for i in range(n_src):
        kn_ref, kr_ref, v_ref = src[3 * i:3 * i + 3]
        n = kn_ref.shape[2]
        lane = lax.broadcasted_iota(jnp.int32, (n, ROPE_TILE), 1)
        keep = (lane // QK_ROPE) == (pl.program_id(1) % 2)
        k_scr[off:off + n, :QK_NOPE] = kn_ref[0, 0]
        k_scr[off:off + n, QK_NOPE:] = jnp.where(keep, kr_ref[0, 0].astype(F32),
                                                 0.0).astype(BF16)
        v_scr[off:off + n, :] = v_ref[0, 0]
        off += n

    for t in range(qn_ref.shape[2] // tq):
        rows = slice(t * tq, (t + 1) * tq)
        q = jnp.concatenate([qn_ref[0, 0, rows, :], qr_ref[0, 0, rows, :]], axis=1)
        s = lax.dot_general(q, k_scr[...], (((1,), (1,)), ((), ())),
                            preferred_element_type=F32)
        e = jnp.exp2(s - jnp.max(s, axis=1, keepdims=True))
        l = jnp.sum(e, axis=1, keepdims=True)
        o = _dot(e.astype(BF16), v_scr[...])
        o_ref[0, 0, rows, :] = (o / l).astype(BF16)


def _attention(q_heads, kv_list, tq):
    nb, _, n_q, _ = q_heads.shape
    slab = lambda n, first, step=1: pl.BlockSpec(
        (1, 1, n, LANE), lambda b, h: (b, first + h // step if step else first, 0, 0))
    in_specs = [slab(n_q, SLAB_QN), slab(n_q, SLAB_QR, ROPE_TILE // QK_ROPE)]
    args = [q_heads, q_heads]
    n_keys = 0
    for kv in kv_list:
        n = kv.shape[2]
        n_keys += n
        in_specs += [slab(n, SLAB_KN), slab(n, SLAB_KR, 0), slab(n, SLAB_V)]
        args += [kv, kv, kv]
    return pl.pallas_call(
        functools.partial(_attn_kernel, n_src=len(kv_list), tq=tq),
        grid=(nb, MLA_HEADS),
        in_specs=in_specs,
        out_specs=pl.BlockSpec((1, 1, n_q, V_DIM), lambda b, h: (b, h, 0, 0)),
        out_shape=jax.ShapeDtypeStruct((nb, MLA_HEADS, n_q, V_DIM), BF16),
        scratch_shapes=[pltpu.VMEM((n_keys, QK_NOPE + ROPE_TILE), BF16),
                        pltpu.VMEM((n_keys, V_DIM), BF16)],
        compiler_params=_params(("arbitrary", "arbitrary")),
        name="attn_%d" % len(kv_list),
    )(*args)


def _merge_kernel(att_ref, rest_ref, x_ref, mod_ref, chan_ref, dft_ref, cw_ref, cb_ref,
                  gn_ref, wout_ref, o_ref, ab_scr, yc_scr, *, seq_len, tr):
    t = pl.program_id(1)

    @pl.when(t == 0)
    def _():
        ab = _dot(rest_ref[0, :, :FNET_W], chan_ref[...])
        ab_scr[0:seq_len, :] = ab[:, :FNET_W].astype(BF16)
        ab_scr[seq_len:2 * seq_len, :] = ab[:, FNET_W:].astype(BF16)
        bg = rest_ref[0, :, FNET_W:FNET_W + CONV_W].astype(F32)
        cg = rest_ref[0, :, FNET_W + CONV_W:FNET_W + 2 * CONV_W].astype(F32)
        xv = rest_ref[0, :, FNET_W + 2 * CONV_W:].astype(F32)
        z = cg * xv
        prev, nxt = _roll_rows(z)
        yc = bg * _conv3_rows(prev, z, nxt, cw_ref.at[0], cb_ref.at[0], seq_len)
        yc_scr[...] = _rms(yc, gn_ref[0, :, MLA_W + FNET_W:]).astype(BF16)

    sub = min(tr, CHAIN_ROWS)
    for c in range(tr // sub):
        rows = slice(c * sub, (c + 1) * sub)
        yf = _dot(dft_ref[rows, :], ab_scr[...])
        yf = _rms(yf, gn_ref[0, :, MLA_W:MLA_W + FNET_W]).astype(BF16)
        att = jnp.concatenate([att_ref[0, i, rows, :] for i in range(MLA_HEADS)], axis=1)
        ya = _rms(att.astype(F32), gn_ref[0, :, :MLA_W]).astype(BF16)
        r0 = pl.multiple_of(t * tr + c * sub, sub)
        y = jnp.concatenate([ya, yf, yc_scr[pl.ds(r0, sub), :]], axis=1)
        o_ref[0, rows, :] = (x_ref[0, rows, :]
                             + mod_ref[0, 0, 2:3, :] * _dot(y, wout_ref[0]))


def _merge(att, rest, x, mod, mod_row, layer, chan_dft, pos_dft, cw, cb, gn, wout, tr):
    nb, seq_len, _ = x.shape
    const = lambda b, t: (0, 0)
    return pl.pallas_call(
        functools.partial(_merge_kernel, seq_len=seq_len, tr=tr),
        grid=(nb, seq_len // tr),
        in_specs=[
            pl.BlockSpec((1, MLA_HEADS, tr, V_DIM), lambda b, t: (b, 0, t, 0)),
            pl.BlockSpec((1, seq_len, REST_COLS), lambda b, t: (b, 0, 0)),
            pl.BlockSpec((1, tr, D_MODEL), lambda b, t: (b, t, 0)),
            _mod_spec(layer, mod_row),
            pl.BlockSpec((FNET_W, 2 * FNET_W), const),
            pl.BlockSpec((tr, 2 * seq_len), lambda b, t: (t, 0)),
            _layer_spec(cw, layer), _layer_spec(cb, layer), _layer_spec(gn, layer),
            _layer_spec(wout, layer),
        ],
        out_specs=pl.BlockSpec((1, tr, D_MODEL), lambda b, t: (b, t, 0)),
        out_shape=jax.ShapeDtypeStruct(x.shape, F32),
        scratch_shapes=[pltpu.VMEM((2 * seq_len, FNET_W), BF16),
                        pltpu.VMEM((seq_len, CONV_W), BF16)],
        compiler_params=_params(("arbitrary", "arbitrary")),
        name="merge_%d" % seq_len,
    )(att, rest, x, mod, chan_dft, pos_dft, cw, cb, gn, wout)


def _ffn_kernel(*refs, seq_len, final, n_chunk):
    if final:
        (x_ref, mod_ref, g2_ref, wg_ref, wv_ref, cw_ref, cb_ref, wd_ref, fg_ref,
         o_ref, h_scr, u_even, u_odd) = refs
    else:
        (x_ref, mod_ref, g2_ref, wg_ref, wv_ref, cw_ref, cb_ref, wd_ref,
         o_ref, h_scr, u_even, u_odd) = refs
    j = pl.program_id(1)
    fc = wg_ref.shape[2]

    def up_project(u_ref):
        w = jnp.concatenate([wg_ref[0].astype(BF16), wv_ref[0].astype(BF16)], axis=1)
        for r in range(0, h_scr.shape[0], UP_ROWS):
            u_ref[r:r + UP_ROWS, :] = _dot(h_scr[r:r + UP_ROWS, :], w)

    def accumulate_down(u_ref):
        z = u_ref[:, :fc]
        prev, nxt = _roll_rows(z)
        gate = _conv3_rows(prev, z, nxt, cw_ref.at[0], cb_ref.at[0], seq_len)
        act = gate * jax.nn.sigmoid(gate) * u_ref[:, fc:]
        o_ref[0] += _dot(act.astype(BF16), wd_ref[0].astype(BF16))

    @pl.when(j == 0)
    def _():
        h = _rms(x_ref[0], g2_ref[0]) * (1.0 + mod_ref[0, 0, 4:5, :]) + mod_ref[0, 0, 3:4, :]
        h_scr[...] = h.astype(BF16)
        o_ref[0] = jnp.zeros(o_ref.shape[1:], F32)
        up_project(u_even)

    steady = (j > 0) & (j < n_chunk)

    @pl.when(steady & (j % 2 == 1))
    def _():
        up_project(u_odd)
        accumulate_down(u_even)

    @pl.when(steady & (j % 2 == 0))
    def _():
        up_project(u_even)
        accumulate_down(u_odd)

    @pl.when(j == n_chunk)
    def _():
        accumulate_down(u_even if (n_chunk - 1) % 2 == 0 else u_odd)
        y = x_ref[0] + mod_ref[0, 0, 5:6, :] * o_ref[0]
        if final:
            y = _rms(y, fg_ref[...])
        o_ref[0] = y


def _ffn(x, mod, mod_row, layer, g2, w_up, cw, cb, w_down, seq_len, final_g, fc):
    nb, n_rows, _ = x.shape
    n_chunk = D_FF // fc
    final = final_g is not None
    const = lambda b, j: (0, 0)
    up = lambda j: jnp.minimum(j, n_chunk - 1)
    down = lambda j: jnp.maximum(j - 1, 0)
    in_specs = [
        pl.BlockSpec((1, n_rows, D_MODEL), lambda b, j: (b, 0, 0)),
        _mod_spec(layer, mod_row),
        _layer_spec(g2, layer),
        pl.BlockSpec((1, D_MODEL, fc), lambda b, j: (layer, 0, up(j))),
        pl.BlockSpec((1, D_MODEL, fc), lambda b, j: (layer, 0, n_chunk + up(j))),
        pl.BlockSpec((1, 3, fc), lambda b, j: (layer, 0, down(j))),
        pl.BlockSpec((1, 1, fc), lambda b, j: (layer, 0, down(j))),
        pl.BlockSpec((1, fc, D_MODEL), lambda b, j: (layer, down(j), 0)),
    ]
    args = [x, mod, g2, w_up, w_up, cw, cb, w_down]
    if final:
        in_specs.append(pl.BlockSpec((1, D_MODEL), const))
        args.append(final_g)
    return pl.pallas_call(
        functools.partial(_ffn_kernel, seq_len=seq_len, final=final, n_chunk=n_chunk),
        grid=(nb, n_chunk + 1),
        in_specs=in_specs,
        out_specs=pl.BlockSpec((1, n_rows, D_MODEL), lambda b, j: (b, 0, 0)),
        out_shape=jax.ShapeDtypeStruct(x.shape, F32),
        scratch_shapes=[pltpu.VMEM((n_rows, D_MODEL), BF16),
                        pltpu.VMEM((n_rows, 2 * fc), F32),
                        pltpu.VMEM((n_rows, 2 * fc), F32)],
        compiler_params=_params(("arbitrary", "arbitrary")),
        name="ffn_final" if final else "ffn_%d" % seq_len,
    )(*args)


def _rope_perm():
    quarter = QK_ROPE // 4
    idx = [a * 2 * quarter + j * quarter + i
           for j in range(2) for a in range(2) for i in range(quarter)]
    return np.asarray(idx, np.int32)


@functools.lru_cache(maxsize=None)
def _rope_table(n_tokens):
    t = np.arange(n_tokens)
    half = QK_ROPE // 2
    inv = ROPE_THETA ** (-np.arange(0, half, 2, dtype=np.float64) / half)
    ang = np.concatenate([(t // GRID_W)[:, None] * inv, (t % GRID_W)[:, None] * inv], axis=1)
    cos, sin, zero = np.cos(ang), np.sin(ang), np.zeros_like(ang)
    per_head = lambda a, b: np.tile(np.concatenate([a, b], axis=1), (1, ROPE_TILE // QK_ROPE))
    tab = np.concatenate([per_head(cos, cos), per_head(zero, sin), per_head(-sin, zero)], axis=1)
    return tab.astype(np.float32)


@functools.lru_cache(maxsize=None)
def _chan_dft():
    gw = FNET_W // FNET_GROUPS
    ang = 2.0 * np.pi * ((np.arange(gw)[:, None] * np.arange(gw)[None, :]) % gw) / gw
    eye = np.eye(FNET_GROUPS)
    return np.concatenate([np.kron(eye, np.cos(ang)), np.kron(eye, np.sin(ang))],
                          axis=1) * gw ** -0.5


@functools.lru_cache(maxsize=None)
def _pos_dft(seq_len):
    ang = 2.0 * np.pi * ((np.arange(seq_len)[:, None] * np.arange(seq_len)[None, :])
                         % seq_len) / seq_len
    return np.concatenate([np.cos(ang), -np.sin(ang)], axis=1) * seq_len ** -0.5


def kernel(x, c, ctx, c_ctx, ada_w, ada_b, norm1_g, w_in, q_norm_g, kv_norm_g, w_uq, w_ukv,
           sconv_w, sconv_b, out_norm_g, w_out, norm2_g, w_up, ffconv_w, ffconv_b, w_down,
           final_g):
    nb, seq, _ = x.shape
    n_ctx = ctx.shape[1]

    cond = jnp.concatenate([c, c_ctx[None, :]], axis=0)
    cond = jnp.pad(cond, ((0, MOD_ROWS - cond.shape[0]), (0, 0)))
    mod = _ada_mod(cond, ada_w, ada_b)

    mod = mod.reshape(DEPTH, MOD_ROWS, 6, D_MODEL)
    lat_row = lambda b: b
    ctx_row = lambda b: nb

    rope_tab = jnp.asarray(_rope_table(seq))
    chan_dft = jnp.asarray(_chan_dft(), F32).astype(BF16)
    pos_dft_lat = jnp.asarray(_pos_dft(seq), F32).astype(BF16)
    pos_dft_ctx = jnp.asarray(_pos_dft(n_ctx), F32).astype(BF16)

    perm = _rope_perm()
    n_lat = Q_LORA + KV_LORA
    k_r = w_in[:, :, n_lat:n_lat + QK_ROPE][:, :, perm]
    win = jnp.concatenate([w_in[:, :, :n_lat], k_r, k_r, w_in[:, :, n_lat + QK_ROPE:]],
                          axis=2).astype(BF16)
    wq = jnp.concatenate([w_uq[..., :QK_NOPE].reshape(DEPTH, Q_LORA, -1),
                          w_uq[..., QK_NOPE:][..., perm].reshape(DEPTH, Q_LORA, -1)],
                         axis=2).astype(BF16)
    wkv = jnp.concatenate([w_ukv[..., :QK_NOPE].reshape(DEPTH, KV_LORA, -1),
                           w_ukv[..., QK_NOPE:].reshape(DEPTH, KV_LORA, -1)],
                          axis=2).astype(BF16)
    stack_row = lambda v: v.reshape(DEPTH, 1, -1)
    proj = (stack_row(norm1_g), win, stack_row(q_norm_g), stack_row(kv_norm_g), wq, wkv)
    mix = (sconv_w, stack_row(sconv_b), stack_row(out_norm_g), w_out.astype(BF16))
    ffn = (stack_row(norm2_g), w_up, ffconv_w, stack_row(ffconv_b), w_down)

    x_lat = x
    x_ctx = ctx
    fc = 256
    for l in range(DEPTH):
        last = l == DEPTH - 1
        heads_l, rest_l = _inproj(x_lat, mod, lat_row, l, *proj, rope_tab, tr=1024)
        heads_c, rest_c = _inproj(x_ctx, mod, ctx_row, l, *proj, None, tr=n_ctx)

        att_l = _attention(heads_l, [heads_c, heads_l], tq=512)
        x_lat = _merge(att_l, rest_l, x_lat, mod, lat_row, l, chan_dft, pos_dft_lat, *mix,
                       tr=1024)
        x_lat = _ffn(x_lat, mod, lat_row, l, *ffn, seq_len=seq,
                     final_g=final_g.reshape(1, -1) if last else None, fc=fc)
        if not last:
            att_c = _attention(heads_c, [heads_c], tq=n_ctx)
            x_ctx = _merge(att_c, rest_c, x_ctx, mod, ctx_row, l, chan_dft, pos_dft_ctx, *mix,
                           tr=n_ctx)
            x_ctx = _ffn(x_ctx.reshape(1, nb * n_ctx, D_MODEL), mod, ctx_row, l, *ffn,
                         seq_len=n_ctx, final_g=None, fc=fc).reshape(nb, n_ctx, D_MODEL)
    return x_lat
```

```python
import functools

import jax
import jax.numpy as jnp
import numpy as np
from jax import lax
from jax.experimental import pallas as pl
from jax.experimental.pallas import tpu as pltpu

D_MODEL = 1024
DEPTH = 2
GRID_W = 64
MLA_HEADS = 4
QK_NOPE = 128
QK_ROPE = 64
V_DIM = 128
Q_LORA = 384
KV_LORA = 256
MLA_W = MLA_HEADS * V_DIM
FNET_GROUPS = 4
FNET_W = 256
CONV_W = 256
D_MIX = MLA_W + FNET_W + CONV_W
D_FF = 2816
ROPE_THETA = 10000.0
EPS = 1e-6
SM_SCALE = (QK_NOPE + QK_ROPE) ** -0.5
Q_SCALE = SM_SCALE * float(np.log2(np.e))

LANE = 128
SUBLANE = 8
BF16_ROWS = 2 * SUBLANE
ROPE_TILE = 2 * QK_ROPE
P_COLS = Q_LORA + KV_LORA + ROPE_TILE + FNET_W + 3 * CONV_W
Q_COLS = MLA_HEADS * QK_NOPE + MLA_HEADS * QK_ROPE
SLAB_QN, SLAB_QR = 0, MLA_HEADS
SLAB_KN = SLAB_QR + MLA_HEADS * QK_ROPE // ROPE_TILE
SLAB_KR = SLAB_KN + MLA_HEADS
SLAB_V = SLAB_KR + 1
N_SLABS = SLAB_V + MLA_HEADS
REST_COLS = FNET_W + 3 * CONV_W
CHAIN_ROWS = 512
UP_ROWS = 1024
MOD_ROWS = 16
VMEM_LIMIT = 62 * 1024 * 1024

BF16 = jnp.bfloat16
F32 = jnp.float32


def _params(sem):
    return pltpu.CompilerParams(dimension_semantics=sem, vmem_limit_bytes=VMEM_LIMIT)


def _rms(x, g):
    return x * lax.rsqrt(jnp.mean(x * x, axis=-1, keepdims=True) + EPS) * g


def _dot(a, b):
    return jnp.dot(a, b, preferred_element_type=F32)


def _roll_rows(z):
    return pltpu.roll(z, 1, axis=0), pltpu.roll(z, z.shape[0] - 1, axis=0)


def _conv3_rows(prev, z, nxt, w_ref, b_ref, seq_len):
    n = z.shape[0]
    w0, w1, w2 = w_ref[0:1, :], w_ref[1:2, :], w_ref[2:3, :]
    out = prev * w0 + z * w1 + nxt * w2 + b_ref[...]
    sub = lax.broadcasted_iota(jnp.int32, (SUBLANE, z.shape[1]), 0)
    pieces, done = [], 0
    for s in range(0, n, seq_len):
        for r, edge, keep_prev, keep_nxt in ((s, 0, False, True),
                                             (s + seq_len - SUBLANE, SUBLANE - 1, True, False)):
            t = slice(r, r + SUBLANE)
            p = prev[t] if keep_prev else jnp.where(sub == edge, 0.0, prev[t])
            q = nxt[t] if keep_nxt else jnp.where(sub == edge, 0.0, nxt[t])
            pieces += [out[done:r], p * w0 + z[t] * w1 + q * w2 + b_ref[...]]
            done = r + SUBLANE
    return jnp.concatenate([x for x in pieces if x.shape[0]], axis=0)


def _ada_kernel(c_ref, w_ref, b_ref, o_ref):
    c = c_ref[...]
    s = (c * jax.nn.sigmoid(c)).astype(BF16)
    o_ref[0] = _dot(s, w_ref[0].astype(BF16)) + b_ref[0]


def _ada_mod(cond, ada_w, ada_b):
    n_col = 6 * D_MODEL
    bc = 1536
    return pl.pallas_call(
        _ada_kernel,
        grid=(DEPTH, n_col // bc),
        in_specs=[
            pl.BlockSpec((MOD_ROWS, D_MODEL), lambda l, j: (0, 0)),
            pl.BlockSpec((1, D_MODEL, bc), lambda l, j: (l, 0, j)),
            pl.BlockSpec((1, 1, bc), lambda l, j: (l, 0, j)),
        ],
        out_specs=pl.BlockSpec((1, MOD_ROWS, bc), lambda l, j: (l, 0, j)),
        out_shape=jax.ShapeDtypeStruct((DEPTH, MOD_ROWS, n_col), F32),
        compiler_params=_params(("arbitrary", "arbitrary")),
        name="ada_mod",
    )(cond, ada_w, ada_b.reshape(DEPTH, 1, n_col))


def _rope_tile(t, tab):
    return (t * tab[:, 0:LANE]
            + pltpu.roll(t, QK_ROPE // 2, axis=1) * tab[:, LANE:2 * LANE]
            + pltpu.roll(t, LANE - QK_ROPE // 2, axis=1) * tab[:, 2 * LANE:3 * LANE])


def _inproj_kernel(*refs, use_rope):
    if use_rope:
        (x_ref, mod_ref, g1_ref, win_ref, qg_ref, kvg_ref, wq_ref, wkv_ref, tab_ref,
         heads_ref, rest_ref) = refs
    else:
        (x_ref, mod_ref, g1_ref, win_ref, qg_ref, kvg_ref, wq_ref, wkv_ref,
         heads_ref, rest_ref) = refs
    n_nope = MLA_HEADS * QK_NOPE
    tr = x_ref.shape[1]
    sub = min(tr, CHAIN_ROWS)
    for c in range(tr // sub):
        rows = slice(c * sub, (c + 1) * sub)
        x = x_ref[0, rows, :]
        h = _rms(x, g1_ref[0]) * (1.0 + mod_ref[0, 0, 1:2, :]) + mod_ref[0, 0, 0:1, :]
        p = _dot(h.astype(BF16), win_ref[0])
        cq = _rms(p[:, :Q_LORA], qg_ref[0]).astype(BF16)
        ckv = _rms(p[:, Q_LORA:Q_LORA + KV_LORA], kvg_ref[0]).astype(BF16)
        q = _dot(cq, wq_ref[0])
        kv = _dot(ckv, wkv_ref[0])
        kr = p[:, Q_LORA + KV_LORA:Q_LORA + KV_LORA + ROPE_TILE]
        qr = [q[:, n_nope + i * LANE:n_nope + (i + 1) * LANE] for i in range(2)]
        if use_rope:
            tab = tab_ref[rows, :]
            kr = _rope_tile(kr, tab)
            qr = [_rope_tile(t, tab) for t in qr]
        slabs = ([q[:, i * LANE:(i + 1) * LANE] * Q_SCALE for i in range(MLA_HEADS)]
                 + [t * Q_SCALE for t in qr]
                 + [kv[:, i * LANE:(i + 1) * LANE] for i in range(MLA_HEADS)]
                 + [kr]
                 + [kv[:, n_nope + i * LANE:n_nope + (i + 1) * LANE] for i in range(MLA_HEADS)])
        for i, slab in enumerate(slabs):
            heads_ref[0, i, rows, :] = slab.astype(BF16)
        rest_ref[0, rows, :] = p[:, Q_LORA + KV_LORA + ROPE_TILE:].astype(BF16)


def _layer_spec(arr, layer):
    zeros = (0,) * (arr.ndim - 1)
    return pl.BlockSpec((1,) + arr.shape[1:], lambda *_: (layer,) + zeros)


def _mod_spec(layer, mod_row):
    return pl.BlockSpec((1, 1, 6, D_MODEL), lambda b, t: (layer, mod_row(b), 0, 0))


def _inproj(x, mod, mod_row, layer, g1, win, qg, kvg, wq, wkv, rope_tab, tr):
    nb, n_rows, _ = x.shape
    use_rope = rope_tab is not None
    params = (g1, win, qg, kvg, wq, wkv)
    in_specs = [pl.BlockSpec((1, tr, D_MODEL), lambda b, t: (b, t, 0)), _mod_spec(layer, mod_row)]
    in_specs += [_layer_spec(p, layer) for p in params]
    args = [x, mod, *params]
    if use_rope:
        in_specs.append(pl.BlockSpec((tr, 3 * LANE), lambda b, t: (t, 0)))
        args.append(rope_tab)
    return pl.pallas_call(
        functools.partial(_inproj_kernel, use_rope=use_rope),
        grid=(nb, n_rows // tr),
        in_specs=in_specs,
        out_specs=[pl.BlockSpec((1, N_SLABS, tr, LANE), lambda b, t: (b, 0, t, 0)),
                   pl.BlockSpec((1, tr, REST_COLS), lambda b, t: (b, t, 0))],
        out_shape=[jax.ShapeDtypeStruct((nb, N_SLABS, n_rows, LANE), BF16),
                   jax.ShapeDtypeStruct((nb, n_rows, REST_COLS), BF16)],
        compiler_params=_params(("arbitrary", "arbitrary")),
        name="inproj_rope" if use_rope else "inproj_ctx",
    )(*args)


def _attn_kernel(*refs, n_src, tq):
    qn_ref, qr_ref = refs[0], refs[1]
    src = refs[2:2 + 3 * n_src]
    o_ref = refs[2 + 3 * n_src]
    k_scr, v_scr, s_scr, p_scr = refs[3 + 3 * n_src:]

    off = 0
    for i in range(n_src):
        kn_ref, kr_ref, v_ref = src[3 * i:3 * i + 3]
        n = kn_ref.shape[2]
        lane = lax.broadcasted_iota(jnp.int32, (n, ROPE_TILE), 1)
        keep = (lane // QK_ROPE) == (pl.program_id(1) % 2)
        k_scr[off:off + n, :QK_NOPE] = kn_ref[0, 0]
        k_scr[off:off + n, QK_NOPE:] = jnp.where(keep, kr_ref[0, 0].astype(F32),
                                                 0.0).astype(BF16)
        v_scr[off:off + n, :] = v_ref[0, 0]
        off += n

    n_tiles = qn_ref.shape[2] // tq

    def scores(t):
        rows = slice(t * tq, (t + 1) * tq)
        q = jnp.concatenate([qn_ref[0, 0, rows, :], qr_ref[0, 0, rows, :]], axis=1)
        s_scr[t % 2] = lax.dot_general(q, k_scr[...], (((1,), (1,)), ((), ())),
                                       preferred_element_type=F32)

    def softmax(t):
        sums = []
        for g in range(0, tq, BF16_ROWS):
            s = s_scr[t % 2, g:g + BF16_ROWS, :]
            e = jnp.exp2(s - jnp.max(s, axis=1, keepdims=True))
            sums.append(jnp.sum(e, axis=1, keepdims=True))
            p_scr[t % 2, g:g + BF16_ROWS, :] = e.astype(BF16)
        return jnp.concatenate(sums, axis=0)

    scores(0)
    for t in range(n_tiles):
        if t + 1 < n_tiles:
            scores(t + 1)
        l = softmax(t)
        o = _dot(p_scr[t % 2], v_scr[...])
        o_ref[0, 0, t * tq:(t + 1) * tq, :] = (o / l).astype(BF16)


def _attention(q_heads, kv_list, tq):
    nb, _, n_q, _ = q_heads.shape
    slab = lambda n, first, step=1: pl.BlockSpec(
        (1, 1, n, LANE), lambda b, h: (b, first + h // step if step else first, 0, 0))
    in_specs = [slab(n_q, SLAB_QN), slab(n_q, SLAB_QR, ROPE_TILE // QK_ROPE)]
    args = [q_heads, q_heads]
    n_keys = 0
    for kv in kv_list:
        n = kv.shape[2]
        n_keys += n
        in_specs += [slab(n, SLAB_KN), slab(n, SLAB_KR, 0), slab(n, SLAB_V)]
        args += [kv, kv, kv]
    return pl.pallas_call(
        functools.partial(_attn_kernel, n_src=len(kv_list), tq=tq),
        grid=(nb, MLA_HEADS),
        in_specs=in_specs,
        out_specs=pl.BlockSpec((1, 1, n_q, V_DIM), lambda b, h: (b, h, 0, 0)),
        out_shape=jax.ShapeDtypeStruct((nb, MLA_HEADS, n_q, V_DIM), BF16),
        scratch_shapes=[pltpu.VMEM((n_keys, QK_NOPE + ROPE_TILE), BF16),
                        pltpu.VMEM((n_keys, V_DIM), BF16),
                        pltpu.VMEM((2, tq, n_keys), F32),
                        pltpu.VMEM((2, tq, n_keys), BF16)],
        compiler_params=_params(("arbitrary", "arbitrary")),
        name="attn_%d" % len(kv_list),
    )(*args)


def _merge_kernel(att_ref, rest_ref, x_ref, mod_ref, chan_ref, dft_ref, cw_ref, cb_ref,
                  gn_ref, wout_ref, o_ref, ab_scr, yc_scr, *, seq_len, tr):
    t = pl.program_id(1)

    @pl.when(t == 0)
    def _():
        ab = _dot(rest_ref[0, :, :FNET_W], chan_ref[...])
        ab_scr[0:seq_len, :] = ab[:, :FNET_W].astype(BF16)
        ab_scr[seq_len:2 * seq_len, :] = ab[:, FNET_W:].astype(BF16)
        bg = rest_ref[0, :, FNET_W:FNET_W + CONV_W].astype(F32)
        cg = rest_ref[0, :, FNET_W + CONV_W:FNET_W + 2 * CONV_W].astype(F32)
        xv = rest_ref[0, :, FNET_W + 2 * CONV_W:].astype(F32)
        z = cg * xv
        prev, nxt = _roll_rows(z)
        yc = bg * _conv3_rows(prev, z, nxt, cw_ref.at[0], cb_ref.at[0], seq_len)
        yc_scr[...] = _rms(yc, gn_ref[0, :, MLA_W + FNET_W:]).astype(BF16)

    sub = min(tr, CHAIN_ROWS)
    for c in range(tr // sub):
        rows = slice(c * sub, (c + 1) * sub)
        yf = _dot(dft_ref[rows, :], ab_scr[...])
        yf = _rms(yf, gn_ref[0, :, MLA_W:MLA_W + FNET_W]).astype(BF16)
        att = jnp.concatenate([att_ref[0, i, rows, :] for i in range(MLA_HEADS)], axis=1)
        ya = _rms(att.astype(F32), gn_ref[0, :, :MLA_W]).astype(BF16)
        r0 = pl.multiple_of(t * tr + c * sub, sub)
        y = jnp.concatenate([ya, yf, yc_scr[pl.ds(r0, sub), :]], axis=1)
        o_ref[0, rows, :] = (x_ref[0, rows, :]
                             + mod_ref[0, 0, 2:3, :] * _dot(y, wout_ref[0]))


def _merge(att, rest, x, mod, mod_row, layer, chan_dft, pos_dft, cw, cb, gn, wout, tr):
    nb, seq_len, _ = x.shape
    const = lambda b, t: (0, 0)
    return pl.pallas_call(
        functools.partial(_merge_kernel, seq_len=seq_len, tr=tr),
        grid=(nb, seq_len // tr),
        in_specs=[
            pl.BlockSpec((1, MLA_HEADS, tr, V_DIM), lambda b, t: (b, 0, t, 0)),
            pl.BlockSpec((1, seq_len, REST_COLS), lambda b, t: (b, 0, 0)),
            pl.BlockSpec((1, tr, D_MODEL), lambda b, t: (b, t, 0)),
            _mod_spec(layer, mod_row),
            pl.BlockSpec((FNET_W, 2 * FNET_W), const),
            pl.BlockSpec((tr, 2 * seq_len), lambda b, t: (t, 0)),
            _layer_spec(cw, layer), _layer_spec(cb, layer), _layer_spec(gn, layer),
            _layer_spec(wout, layer),
        ],
        out_specs=pl.BlockSpec((1, tr, D_MODEL), lambda b, t: (b, t, 0)),
        out_shape=jax.ShapeDtypeStruct(x.shape, F32),
        scratch_shapes=[pltpu.VMEM((2 * seq_len, FNET_W), BF16),
                        pltpu.VMEM((seq_len, CONV_W), BF16)],
        compiler_params=_params(("arbitrary", "arbitrary")),
        name="merge_%d" % seq_len,
    )(att, rest, x, mod, chan_dft, pos_dft, cw, cb, gn, wout)


def _ffn_kernel(*refs, seq_len, final, n_chunk):
    if final:
        (x_ref, mod_ref, g2_ref, wg_ref, wv_ref, cw_ref, cb_ref, wd_ref, fg_ref,
         o_ref, h_scr, u_even, u_odd) = refs
    else:
        (x_ref, mod_ref, g2_ref, wg_ref, wv_ref, cw_ref, cb_ref, wd_ref,
         o_ref, h_scr, u_even, u_odd) = refs
    j = pl.program_id(1)
    fc = wg_ref.shape[2]

    def up_project(u_ref):
        w = jnp.concatenate([wg_ref[0].astype(BF16), wv_ref[0].astype(BF16)], axis=1)
        for r in range(0, h_scr.shape[0], UP_ROWS):
            u_ref[r:r + UP_ROWS, :] = _dot(h_scr[r:r + UP_ROWS, :], w)

    def accumulate_down(u_ref):
        z = u_ref[:, :fc]
        prev, nxt = _roll_rows(z)
        gate = _conv3_rows(prev, z, nxt, cw_ref.at[0], cb_ref.at[0], seq_len)
        act = gate * jax.nn.sigmoid(gate) * u_ref[:, fc:]
        o_ref[0] += _dot(act.astype(BF16), wd_ref[0].astype(BF16))

    @pl.when(j == 0)
    def _():
        h = _rms(x_ref[0], g2_ref[0]) * (1.0 + mod_ref[0, 0, 4:5, :]) + mod_ref[0, 0, 3:4, :]
        h_scr[...] = h.astype(BF16)
        o_ref[0] = jnp.zeros(o_ref.shape[1:], F32)
        up_project(u_even)

    steady = (j > 0) & (j < n_chunk)

    @pl.when(steady & (j % 2 == 1))
    def _():
        up_project(u_odd)
        accumulate_down(u_even)

    @pl.when(steady & (j % 2 == 0))
    def _():
        up_project(u_even)
        accumulate_down(u_odd)

    @pl.when(j == n_chunk)
    def _():
        accumulate_down(u_even if (n_chunk - 1) % 2 == 0 else u_odd)
        y = x_ref[0] + mod_ref[0, 0, 5:6, :] * o_ref[0]
        if final:
            y = _rms(y, fg_ref[...])
        o_ref[0] = y


def _ffn(x, mod, mod_row, layer, g2, w_up, cw, cb, w_down, seq_len, final_g, fc):
    nb, n_rows, _ = x.shape
    n_chunk = D_FF // fc
    final = final_g is not None
    const = lambda b, j: (0, 0)
    up = lambda j: jnp.minimum(j, n_chunk - 1)
    down = lambda j: jnp.maximum(j - 1, 0)
    in_specs = [
        pl.BlockSpec((1, n_rows, D_MODEL), lambda b, j: (b, 0, 0)),
        _mod_spec(layer, mod_row),
        _layer_spec(g2, layer),
        pl.BlockSpec((1, D_MODEL, fc), lambda b, j: (layer, 0, up(j))),
        pl.BlockSpec((1, D_MODEL, fc), lambda b, j: (layer, 0, n_chunk + up(j))),
        pl.BlockSpec((1, 3, fc), lambda b, j: (layer, 0, down(j))),
        pl.BlockSpec((1, 1, fc), lambda b, j: (layer, 0, down(j))),
        pl.BlockSpec((1, fc, D_MODEL), lambda b, j: (layer, down(j), 0)),
    ]
    args = [x, mod, g2, w_up, w_up, cw, cb, w_down]
    if final:
        in_specs.append(pl.BlockSpec((1, D_MODEL), const))
        args.append(final_g)
    return pl.pallas_call(
        functools.partial(_ffn_kernel, seq_len=seq_len, final=final, n_chunk=n_chunk),
        grid=(nb, n_chunk + 1),
        in_specs=in_specs,
        out_specs=pl.BlockSpec((1, n_rows, D_MODEL), lambda b, j: (b, 0, 0)),
        out_shape=jax.ShapeDtypeStruct(x.shape, F32),
        scratch_shapes=[pltpu.VMEM((n_rows, D_MODEL), BF16),
                        pltpu.VMEM((n_rows, 2 * fc), F32),
                        pltpu.VMEM((n_rows, 2 * fc), F32)],
        compiler_params=_params(("arbitrary", "arbitrary")),
        name="ffn_final" if final else "ffn_%d" % seq_len,
    )(*args)


def _rope_perm():
    quarter = QK_ROPE // 4
    idx = [a * 2 * quarter + j * quarter + i
           for j in range(2) for a in range(2) for i in range(quarter)]
    return np.asarray(idx, np.int32)


@functools.lru_cache(maxsize=None)
def _rope_table(n_tokens):
    t = np.arange(n_tokens)
    half = QK_ROPE // 2
    inv = ROPE_THETA ** (-np.arange(0, half, 2, dtype=np.float64) / half)
    ang = np.concatenate([(t // GRID_W)[:, None] * inv, (t % GRID_W)[:, None] * inv], axis=1)
    cos, sin, zero = np.cos(ang), np.sin(ang), np.zeros_like(ang)
    per_head = lambda a, b: np.tile(np.concatenate([a, b], axis=1), (1, ROPE_TILE // QK_ROPE))
    tab = np.concatenate([per_head(cos, cos), per_head(zero, sin), per_head(-sin, zero)], axis=1)
    return tab.astype(np.float32)


@functools.lru_cache(maxsize=None)
def _chan_dft():
    gw = FNET_W // FNET_GROUPS
    ang = 2.0 * np.pi * ((np.arange(gw)[:, None] * np.arange(gw)[None, :]) % gw) / gw
    eye = np.eye(FNET_GROUPS)
    return np.concatenate([np.kron(eye, np.cos(ang)), np.kron(eye, np.sin(ang))],
                          axis=1) * gw ** -0.5


@functools.lru_cache(maxsize=None)
def _pos_dft(seq_len):
    ang = 2.0 * np.pi * ((np.arange(seq_len)[:, None] * np.arange(seq_len)[None, :])
                         % seq_len) / seq_len
    return np.concatenate([np.cos(ang), -np.sin(ang)], axis=1) * seq_len ** -0.5


def kernel(x, c, ctx, c_ctx, ada_w, ada_b, norm1_g, w_in, q_norm_g, kv_norm_g, w_uq, w_ukv,
           sconv_w, sconv_b, out_norm_g, w_out, norm2_g, w_up, ffconv_w, ffconv_b, w_down,
           final_g):
    nb, seq, _ = x.shape
    n_ctx = ctx.shape[1]

    cond = jnp.concatenate([c, c_ctx[None, :]], axis=0)
    cond = jnp.pad(cond, ((0, MOD_ROWS - cond.shape[0]), (0, 0)))
    mod = _ada_mod(cond, ada_w, ada_b)

    mod = mod.reshape(DEPTH, MOD_ROWS, 6, D_MODEL)
    lat_row = lambda b: b
    ctx_row = lambda b: nb

    rope_tab = jnp.asarray(_rope_table(seq))
    chan_dft = jnp.asarray(_chan_dft(), F32).astype(BF16)
    pos_dft_lat = jnp.asarray(_pos_dft(seq), F32).astype(BF16)
    pos_dft_ctx = jnp.asarray(_pos_dft(n_ctx), F32).astype(BF16)

    perm = _rope_perm()
    n_lat = Q_LORA + KV_LORA
    k_r = w_in[:, :, n_lat:n_lat + QK_ROPE][:, :, perm]
    win = jnp.concatenate([w_in[:, :, :n_lat], k_r, k_r, w_in[:, :, n_lat + QK_ROPE:]],
                          axis=2).astype(BF16)
    wq = jnp.concatenate([w_uq[..., :QK_NOPE].reshape(DEPTH, Q_LORA, -1),
                          w_uq[..., QK_NOPE:][..., perm].reshape(DEPTH, Q_LORA, -1)],
                         axis=2).astype(BF16)
    wkv = jnp.concatenate([w_ukv[..., :QK_NOPE].reshape(DEPTH, KV_LORA, -1),
                           w_ukv[..., QK_NOPE:].reshape(DEPTH, KV_LORA, -1)],
                          axis=2).astype(BF16)
    stack_row = lambda v: v.reshape(DEPTH, 1, -1)
    proj = (stack_row(norm1_g), win, stack_row(q_norm_g), stack_row(kv_norm_g), wq, wkv)
    mix = (sconv_w, stack_row(sconv_b), stack_row(out_norm_g), w_out.astype(BF16))
    ffn = (stack_row(norm2_g), w_up, ffconv_w, stack_row(ffconv_b), w_down)

    x_lat = x
    x_ctx = ctx
    fc = 256
    for l in range(DEPTH):
        last = l == DEPTH - 1
        heads_l, rest_l = _inproj(x_lat, mod, lat_row, l, *proj, rope_tab, tr=1024)
        heads_c, rest_c = _inproj(x_ctx, mod, ctx_row, l, *proj, None, tr=n_ctx)

        att_l = _attention(heads_l, [heads_c, heads_l], tq=512)
        x_lat = _merge(att_l, rest_l, x_lat, mod, lat_row, l, chan_dft, pos_dft_lat, *mix,
                       tr=1024)
        x_lat = _ffn(x_lat, mod, lat_row, l, *ffn, seq_len=seq,
                     final_g=final_g.reshape(1, -1) if last else None, fc=fc)
        if not last:
            att_c = _attention(heads_c, [heads_c], tq=n_ctx)
            x_ctx = _merge(att_c, rest_c, x_ctx, mod, ctx_row, l, chan_dft, pos_dft_ctx, *mix,
                           tr=n_ctx)
            x_ctx = _ffn(x_ctx.reshape(1, nb * n_ctx, D_MODEL), mod, ctx_row, l, *ffn,
                         seq_len=n_ctx, final_g=None, fc=fc).reshape(nb, n_ctx, D_MODEL)
    return x_lat
```

```python
import functools

import jax
import jax.numpy as jnp
import numpy as np
from jax import lax
from jax.experimental import pallas as pl
from jax.experimental.pallas import tpu as pltpu

D_MODEL = 1024
DEPTH = 2
GRID_W = 64
MLA_HEADS = 4
QK_NOPE = 128
QK_ROPE = 64
V_DIM = 128
Q_LORA = 384
KV_LORA = 256
MLA_W = MLA_HEADS * V_DIM
FNET_GROUPS = 4
FNET_W = 256
CONV_W = 256
D_MIX = MLA_W + FNET_W + CONV_W
D_FF = 2816
ROPE_THETA = 10000.0
EPS = 1e-6
SM_SCALE = (QK_NOPE + QK_ROPE) ** -0.5
Q_SCALE = SM_SCALE * float(np.log2(np.e))

LANE = 128
SUBLANE = 8
BF16_ROWS = 2 * SUBLANE
ROPE_TILE = 2 * QK_ROPE
P_COLS = Q_LORA + KV_LORA + ROPE_TILE + FNET_W + 3 * CONV_W
Q_COLS = MLA_HEADS * QK_NOPE + MLA_HEADS * QK_ROPE
SLAB_QN, SLAB_QR = 0, MLA_HEADS
SLAB_KN = SLAB_QR + MLA_HEADS * QK_ROPE // ROPE_TILE
SLAB_KR = SLAB_KN + MLA_HEADS
SLAB_V = SLAB_KR + 1
N_SLABS = SLAB_V + MLA_HEADS
REST_COLS = FNET_W + 3 * CONV_W
CHAIN_ROWS = 512
UP_ROWS = 1024
MOD_ROWS = 16
VMEM_LIMIT = 62 * 1024 * 1024

BF16 = jnp.bfloat16
F32 = jnp.float32


def _params(sem):
    return pltpu.CompilerParams(dimension_semantics=sem, vmem_limit_bytes=VMEM_LIMIT)


def _rms(x, g):
    return x * lax.rsqrt(jnp.mean(x * x, axis=-1, keepdims=True) + EPS) * g


def _dot(a, b):
    return jnp.dot(a, b, preferred_element_type=F32)


def _roll_rows(z):
    return pltpu.roll(z, 1, axis=0), pltpu.roll(z, z.shape[0] - 1, axis=0)


def _conv3_rows(prev, z, nxt, w_ref, b_ref, seq_len):
    n = z.shape[0]
    w0, w1, w2 = w_ref[0:1, :], w_ref[1:2, :], w_ref[2:3, :]
    out = prev * w0 + z * w1 + nxt * w2 + b_ref[...]
    sub = lax.broadcasted_iota(jnp.int32, (SUBLANE, z.shape[1]), 0)
    pieces, done = [], 0
    for s in range(0, n, seq_len):
        for r, edge, keep_prev, keep_nxt in ((s, 0, False, True),
                                             (s + seq_len - SUBLANE, SUBLANE - 1, True, False)):
            t = slice(r, r + SUBLANE)
            p = prev[t] if keep_prev else jnp.where(sub == edge, 0.0, prev[t])
            q = nxt[t] if keep_nxt else jnp.where(sub == edge, 0.0, nxt[t])
            pieces += [out[done:r], p * w0 + z[t] * w1 + q * w2 + b_ref[...]]
            done = r + SUBLANE
    return jnp.concatenate([x for x in pieces if x.shape[0]], axis=0)


def _ada_kernel(c_ref, w_ref, b_ref, o_ref):
    c = c_ref[...]
    s = (c * jax.nn.sigmoid(c)).astype(BF16)
    o_ref[0] = _dot(s, w_ref[0].astype(BF16)) + b_ref[0]


def _ada_mod(cond, ada_w, ada_b):
    n_col = 6 * D_MODEL
    bc = 1536
    return pl.pallas_call(
        _ada_kernel,
        grid=(DEPTH, n_col // bc),
        in_specs=[
            pl.BlockSpec((MOD_ROWS, D_MODEL), lambda l, j: (0, 0)),
            pl.BlockSpec((1, D_MODEL, bc), lambda l, j: (l, 0, j)),
            pl.BlockSpec((1, 1, bc), lambda l, j: (l, 0, j)),
        ],
        out_specs=pl.BlockSpec((1, MOD_ROWS, bc), lambda l, j: (l, 0, j)),
        out_shape=jax.ShapeDtypeStruct((DEPTH, MOD_ROWS, n_col), F32),
        compiler_params=_params(("arbitrary", "arbitrary")),
        name="ada_mod",
    )(cond, ada_w, ada_b.reshape(DEPTH, 1, n_col))


def _rope_tile(t, tab):
    return (t * tab[:, 0:LANE]
            + pltpu.roll(t, QK_ROPE // 2, axis=1) * tab[:, LANE:2 * LANE]
            + pltpu.roll(t, LANE - QK_ROPE // 2, axis=1) * tab[:, 2 * LANE:3 * LANE])


def _inproj_kernel(*refs, use_rope):
    if use_rope:
        (x_ref, mod_ref, g1_ref, win_ref, qg_ref, kvg_ref, wq_ref, wkv_ref, tab_ref,
         heads_ref, rest_ref) = refs
    else:
        (x_ref, mod_ref, g1_ref, win_ref, qg_ref, kvg_ref, wq_ref, wkv_ref,
         heads_ref, rest_ref) = refs
    n_nope = MLA_HEADS * QK_NOPE
    tr = x_ref.shape[1]
    sub = min(tr, CHAIN_ROWS)
    for c in range(tr // sub):
        rows = slice(c * sub, (c + 1) * sub)
        x = x_ref[0, rows, :]
        h = _rms(x, g1_ref[0]) * (1.0 + mod_ref[0, 0, 1:2, :]) + mod_ref[0, 0, 0:1, :]
        p = _dot(h.astype(BF16), win_ref[0])
        cq = _rms(p[:, :Q_LORA], qg_ref[0]).astype(BF16)
        ckv = _rms(p[:, Q_LORA:Q_LORA + KV_LORA], kvg_ref[0]).astype(BF16)
        q = _dot(cq, wq_ref[0])
        kv = _dot(ckv, wkv_ref[0])
        kr = p[:, Q_LORA + KV_LORA:Q_LORA + KV_LORA + ROPE_TILE]
        qr = [q[:, n_nope + i * LANE:n_nope + (i + 1) * LANE] for i in range(2)]
        if use_rope:
            tab = tab_ref[rows, :]
            kr = _rope_tile(kr, tab)
            qr = [_rope_tile(t, tab) for t in qr]
        slabs = ([q[:, i * LANE:(i + 1) * LANE] * Q_SCALE for i in range(MLA_HEADS)]
                 + [t * Q_SCALE for t in qr]
                 + [kv[:, i * LANE:(i + 1) * LANE] for i in range(MLA_HEADS)]
                 + [kr]
                 + [kv[:, n_nope + i * LANE:n_nope + (i + 1) * LANE] for i in range(MLA_HEADS)])
        for i, slab in enumerate(slabs):
            heads_ref[0, i, rows, :] = slab.astype(BF16)
        rest_ref[0, rows, :] = p[:, Q_LORA + KV_LORA + ROPE_TILE:].astype(BF16)


def _layer_spec(arr, layer):
    zeros = (0,) * (arr.ndim - 1)
    return pl.BlockSpec((1,) + arr.shape[1:], lambda *_: (layer,) + zeros)


def _mod_spec(layer, mod_row):
    return pl.BlockSpec((1, 1, 6, D_MODEL), lambda b, t: (layer, mod_row(b), 0, 0))


def _inproj(x, mod, mod_row, layer, g1, win, qg, kvg, wq, wkv, rope_tab, tr):
    nb, n_rows, _ = x.shape
    use_rope = rope_tab is not None
    params = (g1, win, qg, kvg, wq, wkv)
    in_specs = [pl.BlockSpec((1, tr, D_MODEL), lambda b, t: (b, t, 0)), _mod_spec(layer, mod_row)]
    in_specs += [_layer_spec(p, layer) for p in params]
    args = [x, mod, *params]
    if use_rope:
        in_specs.append(pl.BlockSpec((tr, 3 * LANE), lambda b, t: (t, 0)))
        args.append(rope_tab)
    return pl.pallas_call(
        functools.partial(_inproj_kernel, use_rope=use_rope),
        grid=(nb, n_rows // tr),
        in_specs=in_specs,
        out_specs=[pl.BlockSpec((1, N_SLABS, tr, LANE), lambda b, t: (b, 0, t, 0)),
                   pl.BlockSpec((1, tr, REST_COLS), lambda b, t: (b, t, 0))],
        out_shape=[jax.ShapeDtypeStruct((nb, N_SLABS, n_rows, LANE), BF16),
                   jax.ShapeDtypeStruct((nb, n_rows, REST_COLS), BF16)],
        compiler_params=_params(("arbitrary", "arbitrary")),
        name="inproj_rope" if use_rope else "inproj_ctx",
    )(*args)


def _attn_kernel(*refs, n_src, tq):
    qn_ref, qr_ref = refs[0], refs[1]
    src = refs[2:2 + 3 * n_src]
    o_ref = refs[2 + 3 * n_src]
    k_scr, v_scr, s_scr, p_scr = refs[3 + 3 * n_src:]

    off = 0
    for i in range(n_src):
        kn_ref, kr_ref, v_ref = src[3 * i:3 * i + 3]
        n = kn_ref.shape[2]
        lane = lax.broadcasted_iota(jnp.int32, (n, ROPE_TILE), 1)
        keep = (lane // QK_ROPE) == (pl.program_id(1) % 2)
        k_scr[off:off + n, :QK_NOPE] = kn_ref[0, 0]
        k_scr[off:off + n, QK_NOPE:] = jnp.where(keep, kr_ref[0, 0].astype(F32),
                                                 0.0).astype(BF16)
        v_scr[off:off + n, :] = v_ref[0, 0]
        off += n

    n_tiles = qn_ref.shape[2] // tq

    def scores(t):
        rows = slice(t * tq, (t + 1) * tq)
        q = jnp.concatenate([qn_ref[0, 0, rows, :], qr_ref[0, 0, rows, :]], axis=1)
        s_scr[t % 2] = lax.dot_general(q, k_scr[...], (((1,), (1,)), ((), ())),
                                       preferred_element_type=F32)

    def softmax(t):
        sums = []
        for g in range(0, tq, BF16_ROWS):
            s = s_scr[t % 2, g:g + BF16_ROWS, :]
            e = jnp.exp2(s - jnp.max(s, axis=1, keepdims=True))
            sums.append(jnp.sum(e, axis=1, keepdims=True))
            p_scr[t % 2, g:g + BF16_ROWS, :] = e.astype(BF16)
        return jnp.concatenate(sums, axis=0)

    scores(0)
    for t in range(n_tiles):
        if t + 1 < n_tiles:
            scores(t + 1)
        l = softmax(t)
        o = _dot(p_scr[t % 2], v_scr[...])
        o_ref[0, 0, t * tq:(t + 1) * tq, :] = (o / l).astype(BF16)


def _attention(q_heads, kv_list, tq):
    nb, _, n_q, _ = q_heads.shape
    slab = lambda n, first, step=1: pl.BlockSpec(
        (1, 1, n, LANE), lambda b, h: (b, first + h // step if step else first, 0, 0))
    in_specs = [slab(n_q, SLAB_QN), slab(n_q, SLAB_QR, ROPE_TILE // QK_ROPE)]
    args = [q_heads, q_heads]
    n_keys = 0
    for kv in kv_list:
        n = kv.shape[2]
        n_keys += n
        in_specs += [slab(n, SLAB_KN), slab(n, SLAB_KR, 0), slab(n, SLAB_V)]
        args += [kv, kv, kv]
    return pl.pallas_call(
        functools.partial(_attn_kernel, n_src=len(kv_list), tq=tq),
        grid=(nb, MLA_HEADS),
        in_specs=in_specs,
        out_specs=pl.BlockSpec((1, 1, n_q, V_DIM), lambda b, h: (b, h, 0, 0)),
        out_shape=jax.ShapeDtypeStruct((nb, MLA_HEADS, n_q, V_DIM), BF16),
        scratch_shapes=[pltpu.VMEM((n_keys, QK_NOPE + ROPE_TILE), BF16),
                        pltpu.VMEM((n_keys, V_DIM), BF16),
                        pltpu.VMEM((2, tq, n_keys), F32),
                        pltpu.VMEM((2, tq, n_keys), BF16)],
        compiler_params=_params(("arbitrary", "arbitrary")),
        name="attn_%d" % len(kv_list),
    )(*args)


def _merge_kernel(att_ref, rest_ref, x_ref, mod_ref, chan_ref, dft_ref, cw_ref, cb_ref,
                  gn_ref, wout_ref, o_ref, ab_scr, yc_scr, *, seq_len, tr):
    t = pl.program_id(1)

    @pl.when(t == 0)
    def _():
        ab = _dot(rest_ref[0, :, :FNET_W], chan_ref[...])
        ab_scr[0:seq_len, :] = ab[:, :FNET_W].astype(BF16)
        ab_scr[seq_len:2 * seq_len, :] = ab[:, FNET_W:].astype(BF16)
        bg = rest_ref[0, :, FNET_W:FNET_W + CONV_W].astype(F32)
        cg = rest_ref[0, :, FNET_W + CONV_W:FNET_W + 2 * CONV_W].astype(F32)
        xv = rest_ref[0, :, FNET_W + 2 * CONV_W:].astype(F32)
        z = cg * xv
        prev, nxt = _roll_rows(z)
        yc = bg * _conv3_rows(prev, z, nxt, cw_ref.at[0], cb_ref.at[0], seq_len)
        yc_scr[...] = _rms(yc, gn_ref[0, :, MLA_W + FNET_W:]).astype(BF16)

    sub = min(tr, CHAIN_ROWS)
    yfs = [_dot(dft_ref[c * sub:(c + 1) * sub, :], ab_scr[...]) for c in range(tr // sub)]
    for c in range(tr // sub):
        rows = slice(c * sub, (c + 1) * sub)
        yf = _rms(yfs[c], gn_ref[0, :, MLA_W:MLA_W + FNET_W]).astype(BF16)
        att = jnp.concatenate([att_ref[0, i, rows, :] for i in range(MLA_HEADS)], axis=1)
        ya = _rms(att.astype(F32), gn_ref[0, :, :MLA_W]).astype(BF16)
        r0 = pl.multiple_of(t * tr + c * sub, sub)
        y = jnp.concatenate([ya, yf, yc_scr[pl.ds(r0, sub), :]], axis=1)
        o_ref[0, rows, :] = (x_ref[0, rows, :]
                             + mod_ref[0, 0, 2:3, :] * _dot(y, wout_ref[0]))


def _merge(att, rest, x, mod, mod_row, layer, chan_dft, pos_dft, cw, cb, gn, wout, tr):
    nb, seq_len, _ = x.shape
    const = lambda b, t: (0, 0)
    return pl.pallas_call(
        functools.partial(_merge_kernel, seq_len=seq_len, tr=tr),
        grid=(nb, seq_len // tr),
        in_specs=[
            pl.BlockSpec((1, MLA_HEADS, tr, V_DIM), lambda b, t: (b, 0, t, 0)),
            pl.BlockSpec((1, seq_len, REST_COLS), lambda b, t: (b, 0, 0)),
            pl.BlockSpec((1, tr, D_MODEL), lambda b, t: (b, t, 0)),
            _mod_spec(layer, mod_row),
            pl.BlockSpec((FNET_W, 2 * FNET_W), const),
            pl.BlockSpec((tr, 2 * seq_len), lambda b, t: (t, 0)),
            _layer_spec(cw, layer), _layer_spec(cb, layer), _layer_spec(gn, layer),
            _layer_spec(wout, layer),
        ],
        out_specs=pl.BlockSpec((1, tr, D_MODEL), lambda b, t: (b, t, 0)),
        out_shape=jax.ShapeDtypeStruct(x.shape, F32),
        scratch_shapes=[pltpu.VMEM((2 * seq_len, FNET_W), BF16),
                        pltpu.VMEM((seq_len, CONV_W), BF16)],
        compiler_params=_params(("arbitrary", "arbitrary")),
        name="merge_%d" % seq_len,
    )(att, rest, x, mod, chan_dft, pos_dft, cw, cb, gn, wout)


def _ffn_kernel(*refs, seq_len, final, n_chunk):
    if final:
        (x_ref, mod_ref, g2_ref, wg_ref, wv_ref, cw_ref, cb_ref, wd_ref, fg_ref,
         o_ref, h_scr, u_even, u_odd) = refs
    else:
        (x_ref, mod_ref, g2_ref, wg_ref, wv_ref, cw_ref, cb_ref, wd_ref,
         o_ref, h_scr, u_even, u_odd) = refs
    j = pl.program_id(1)
    fc = wg_ref.shape[2]

    def up_project(u_ref):
        w = jnp.concatenate([wg_ref[0].astype(BF16), wv_ref[0].astype(BF16)], axis=1)
        for r in range(0, h_scr.shape[0], UP_ROWS):
            u_ref[r:r + UP_ROWS, :] = _dot(h_scr[r:r + UP_ROWS, :], w)

    def accumulate_down(u_ref):
        z = u_ref[:, :fc]
        prev, nxt = _roll_rows(z)
        gate = _conv3_rows(prev, z, nxt, cw_ref.at[0], cb_ref.at[0], seq_len)
        act = gate * jax.nn.sigmoid(gate) * u_ref[:, fc:]
        o_ref[0] += _dot(act.astype(BF16), wd_ref[0].astype(BF16))

    @pl.when(j == 0)
    def _():
        h = _rms(x_ref[0], g2_ref[0]) * (1.0 + mod_ref[0, 0, 4:5, :]) + mod_ref[0, 0, 3:4, :]
        h_scr[...] = h.astype(BF16)
        o_ref[0] = jnp.zeros(o_ref.shape[1:], F32)
        up_project(u_even)

    steady = (j > 0) & (j < n_chunk)

    @pl.when(steady & (j % 2 == 1))
    def _():
        up_project(u_odd)
        accumulate_down(u_even)

    @pl.when(steady & (j % 2 == 0))
    def _():
        up_project(u_even)
        accumulate_down(u_odd)

    @pl.when(j == n_chunk)
    def _():
        accumulate_down(u_even if (n_chunk - 1) % 2 == 0 else u_odd)
        y = x_ref[0] + mod_ref[0, 0, 5:6, :] * o_ref[0]
        if final:
            y = _rms(y, fg_ref[...])
        o_ref[0] = y


def _ffn(x, mod, mod_row, layer, g2, w_up, cw, cb, w_down, seq_len, final_g, fc):
    nb, n_rows, _ = x.shape
    n_chunk = D_FF // fc
    final = final_g is not None
    const = lambda b, j: (0, 0)
    up = lambda j: jnp.minimum(j, n_chunk - 1)
    down = lambda j: jnp.maximum(j - 1, 0)
    in_specs = [
        pl.BlockSpec((1, n_rows, D_MODEL), lambda b, j: (b, 0, 0)),
        _mod_spec(layer, mod_row),
        _layer_spec(g2, layer),
        pl.BlockSpec((1, D_MODEL, fc), lambda b, j: (layer, 0, up(j))),
        pl.BlockSpec((1, D_MODEL, fc), lambda b, j: (layer, 0, n_chunk + up(j))),
        pl.BlockSpec((1, 3, fc), lambda b, j: (layer, 0, down(j))),
        pl.BlockSpec((1, 1, fc), lambda b, j: (layer, 0, down(j))),
        pl.BlockSpec((1, fc, D_MODEL), lambda b, j: (layer, down(j), 0)),
    ]
    args = [x, mod, g2, w_up, w_up, cw, cb, w_down]
    if final:
        in_specs.append(pl.BlockSpec((1, D_MODEL), const))
        args.append(final_g)
    return pl.pallas_call(
        functools.partial(_ffn_kernel, seq_len=seq_len, final=final, n_chunk=n_chunk),
        grid=(nb, n_chunk + 1),
        in_specs=in_specs,
        out_specs=pl.BlockSpec((1, n_rows, D_MODEL), lambda b, j: (b, 0, 0)),
        out_shape=jax.ShapeDtypeStruct(x.shape, F32),
        scratch_shapes=[pltpu.VMEM((n_rows, D_MODEL), BF16),
                        pltpu.VMEM((n_rows, 2 * fc), F32),
                        pltpu.VMEM((n_rows, 2 * fc), F32)],
        compiler_params=_params(("arbitrary", "arbitrary")),
        name="ffn_final" if final else "ffn_%d" % seq_len,
    )(*args)


def _rope_perm():
    quarter = QK_ROPE // 4
    idx = [a * 2 * quarter + j * quarter + i
           for j in range(2) for a in range(2) for i in range(quarter)]
    return np.asarray(idx, np.int32)


@functools.lru_cache(maxsize=None)
def _rope_table(n_tokens):
    t = np.arange(n_tokens)
    half = QK_ROPE // 2
    inv = ROPE_THETA ** (-np.arange(0, half, 2, dtype=np.float64) / half)
    ang = np.concatenate([(t // GRID_W)[:, None] * inv, (t % GRID_W)[:, None] * inv], axis=1)
    cos, sin, zero = np.cos(ang), np.sin(ang), np.zeros_like(ang)
    per_head = lambda a, b: np.tile(np.concatenate([a, b], axis=1), (1, ROPE_TILE // QK_ROPE))
    tab = np.concatenate([per_head(cos, cos), per_head(zero, sin), per_head(-sin, zero)], axis=1)
    return tab.astype(np.float32)


@functools.lru_cache(maxsize=None)
def _chan_dft():
    gw = FNET_W // FNET_GROUPS
    ang = 2.0 * np.pi * ((np.arange(gw)[:, None] * np.arange(gw)[None, :]) % gw) / gw
    eye = np.eye(FNET_GROUPS)
    return np.concatenate([np.kron(eye, np.cos(ang)), np.kron(eye, np.sin(ang))],
                          axis=1) * gw ** -0.5


@functools.lru_cache(maxsize=None)
def _pos_dft(seq_len):
    ang = 2.0 * np.pi * ((np.arange(seq_len)[:, None] * np.arange(seq_len)[None, :])
                         % seq_len) / seq_len
    return np.concatenate([np.cos(ang), -np.sin(ang)], axis=1) * seq_len ** -0.5


def kernel(x, c, ctx, c_ctx, ada_w, ada_b, norm1_g, w_in, q_norm_g, kv_norm_g, w_uq, w_ukv,
           sconv_w, sconv_b, out_norm_g, w_out, norm2_g, w_up, ffconv_w, ffconv_b, w_down,
           final_g):
    nb, seq, _ = x.shape
    n_ctx = ctx.shape[1]

    cond = jnp.concatenate([c, c_ctx[None, :]], axis=0)
    cond = jnp.pad(cond, ((0, MOD_ROWS - cond.shape[0]), (0, 0)))
    mod = _ada_mod(cond, ada_w, ada_b)

    mod = mod.reshape(DEPTH, MOD_ROWS, 6, D_MODEL)
    lat_row = lambda b: b
    ctx_row = lambda b: nb

    rope_tab = jnp.asarray(_rope_table(seq))
    chan_dft = jnp.asarray(_chan_dft(), F32).astype(BF16)
    pos_dft_lat = jnp.asarray(_pos_dft(seq), F32).astype(BF16)
    pos_dft_ctx = jnp.asarray(_pos_dft(n_ctx), F32).astype(BF16)

    perm = _rope_perm()
    n_lat = Q_LORA + KV_LORA
    k_r = w_in[:, :, n_lat:n_lat + QK_ROPE][:, :, perm]
    win = jnp.concatenate([w_in[:, :, :n_lat], k_r, k_r, w_in[:, :, n_lat + QK_ROPE:]],
                          axis=2).astype(BF16)
    wq = jnp.concatenate([w_uq[..., :QK_NOPE].reshape(DEPTH, Q_LORA, -1),
                          w_uq[..., QK_NOPE:][..., perm].reshape(DEPTH, Q_LORA, -1)],
                         axis=2).astype(BF16)
    wkv = jnp.concatenate([w_ukv[..., :QK_NOPE].reshape(DEPTH, KV_LORA, -1),
                           w_ukv[..., QK_NOPE:].reshape(DEPTH, KV_LORA, -1)],
                          axis=2).astype(BF16)
    stack_row = lambda v: v.reshape(DEPTH, 1, -1)
    proj = (stack_row(norm1_g), win, stack_row(q_norm_g), stack_row(kv_norm_g), wq, wkv)
    mix = (sconv_w, stack_row(sconv_b), stack_row(out_norm_g), w_out.astype(BF16))
    ffn = (stack_row(norm2_g), w_up, ffconv_w, stack_row(ffconv_b), w_down)

    x_lat = x
    x_ctx = ctx
    fc = 256
    for l in range(DEPTH):
        last = l == DEPTH - 1
        heads_l, rest_l = _inproj(x_lat, mod, lat_row, l, *proj, rope_tab, tr=1024)
        heads_c, rest_c = _inproj(x_ctx, mod, ctx_row, l, *proj, None, tr=n_ctx)

        att_l = _attention(heads_l, [heads_c, heads_l], tq=512)
        x_lat = _merge(att_l, rest_l, x_lat, mod, lat_row, l, chan_dft, pos_dft_lat, *mix,
                       tr=1024)
        x_lat = _ffn(x_lat, mod, lat_row, l, *ffn, seq_len=seq,
                     final_g=final_g.reshape(1, -1) if last else None, fc=fc)
        if not last:
            att_c = _attention(heads_c, [heads_c], tq=n_ctx)
            x_ctx = _merge(att_c, rest_c, x_ctx, mod, ctx_row, l, chan_dft, pos_dft_ctx, *mix,
                           tr=n_ctx)
            x_ctx = _ffn(x_ctx.reshape(1, nb * n_ctx, D_MODEL), mod, ctx_row, l, *ffn,
                         seq_len=n_ctx, final_g=None, fc=fc).reshape(nb, n_ctx, D_MODEL)
    return x_lat
```
